```python
import jax, jax.numpy as jnp
from jax import lax
import numpy as np

D_MODEL = 1024
BATCH = 8
SEQ = 2048
DEPTH = 2
DEC_BATCH = 128
DEC_SEQ = 4
PAST_LEN = 2048
PAGE_SIZE = 128

N_META = 16
N_A_LAYERS = (DEPTH + 1) // 2
N_C_LAYERS = DEPTH // 2
W_A = D_MODEL // 2
H_A = 8
DH_A = W_A // H_A
Q_BLOCK = 128
SB_BIAS_INIT = -6.0
W_B = D_MODEL // 2
POOL_WINDOWS = (2, 4, 8, 16)
POOL_GROUPS = len(POOL_WINDOWS)
POOL_GC = W_B // POOL_GROUPS
POOL_MAX = max(POOL_WINDOWS)
INNER_C = 2 * D_MODEL
H_C = 4
DH_C = INNER_C // H_C
CONV_W = 4
CHUNK = 128
EPS = 1e-6
F32 = jnp.float32

kernel_name = 'hybrid_stickbreak_pool_mlstm_decoder_step'


def rmsnorm(x, g):
    xf = x.astype(F32)
    y = xf * lax.rsqrt(jnp.mean(xf * xf, axis=-1, keepdims=True) + EPS)
    return (y * g.astype(F32)).astype(x.dtype)


def sb_block(q, k, v, bias, q_pos, k_pos):
    z = jnp.einsum('bqhd,bkhd->bhqk', q.astype(F32), k.astype(F32)) * (DH_A ** -0.5)
    z = z + bias.astype(F32)[None, :, None, None]
    mask = k_pos[None, :] < q_pos[:, None]
    lneg = jnp.where(mask, jax.nn.log_sigmoid(-z), 0.0)
    tail = lax.cumsum(lneg, axis=3, reverse=True) - lneg
    a = jnp.where(mask, jnp.exp(jax.nn.log_sigmoid(z) + tail), 0.0)
    return jnp.einsum('bhqk,bkhd->bqhd', a, v.astype(F32))


def sb_prompt(q, k, v, bias):
    B, L = q.shape[0], q.shape[1]
    pos = jnp.arange(L)
    o_meta = sb_block(q[:, :N_META], k[:, :N_META], v[:, :N_META], bias, pos[:N_META], pos[:N_META])
    nb = (L - N_META) // Q_BLOCK
    qb = jnp.moveaxis(q[:, N_META:].reshape(B, nb, Q_BLOCK, H_A, DH_A), 1, 0)
    pb = pos[N_META:].reshape(nb, Q_BLOCK)
    o_real = lax.map(lambda a: sb_block(a[0], k, v, bias, a[1], pos), (qb, pb))
    o_real = jnp.moveaxis(o_real, 0, 1).reshape(B, L - N_META, H_A, DH_A)
    return jnp.concatenate([o_meta, o_real], axis=1)


def pool_mix(u_ext, pos, w_pool, scale):
    P = POOL_MAX - 1
    B, Lx, _ = u_ext.shape
    T = Lx - P
    uf = u_ext.astype(F32)
    cs = jnp.concatenate([jnp.zeros((B, 1, W_B), F32), jnp.cumsum(uf, axis=1)], axis=1)
    means = []
    for g, w in enumerate(POOL_WINDOWS):
        c0 = g * POOL_GC
        win = cs[:, P + 1:P + 1 + T, c0:c0 + POOL_GC] - cs[:, P + 1 - w:P + 1 - w + T, c0:c0 + POOL_GC]
        cnt = jnp.minimum(w, pos + 1).astype(F32)
        means.append(win / cnt[None, :, None])
    mean = jnp.stack(means, axis=2)
    diff = mean - uf[:, P:].reshape(B, T, POOL_GROUPS, POOL_GC)
    y = jnp.einsum('btgc,gce->btge', diff, w_pool.astype(F32)).reshape(B, T, W_B)
    return y * scale.astype(F32)


def even_mixer(xn, pos, pool_prefix, sb_fn, w_in, w_out, sb_bias, w_pool, pool_scale):
    B, T, _ = xn.shape
    proj = xn @ w_in
    q = proj[..., 0:W_A].reshape(B, T, H_A, DH_A)
    k = proj[..., W_A:2 * W_A].reshape(B, T, H_A, DH_A)
    v = proj[..., 2 * W_A:3 * W_A].reshape(B, T, H_A, DH_A)
    g_a = proj[..., 3 * W_A:4 * W_A].astype(F32)
    u = proj[..., 4 * W_A:4 * W_A + W_B]
    g_b = proj[..., 4 * W_A + W_B:].astype(F32)
    o_a = sb_fn(q, k, v, sb_bias).reshape(B, T, W_A)
    u_ext = jnp.concatenate([pool_prefix.astype(u.dtype), u], axis=1)
    o_b = pool_mix(u_ext, pos, w_pool, pool_scale)
    mixed = jnp.concatenate([o_a * jax.nn.silu(g_a), o_b * jax.nn.silu(g_b)], axis=-1).astype(xn.dtype)
    return mixed @ w_out, k, v, u_ext[:, -(POOL_MAX - 1):]


def mlstm_chunk(C, n, m, q, k, v, logi, logf):
    C = C.astype(F32)
    n = n.astype(F32)
    m = m.astype(F32)
    T = q.shape[2]
    b = jnp.cumsum(logf, axis=-1)
    causal = jnp.tril(jnp.ones((T, T), dtype=bool))
    d = jnp.where(causal, b[..., :, None] - b[..., None, :] + logi[..., None, :], -jnp.inf)
    m_inter = b + m[..., None]
    m_t = jnp.maximum(m_inter, jnp.max(d, axis=-1))
    w_inter = jnp.exp(m_inter - m_t)
    s = jnp.einsum('bhtd,bhsd->bhts', q, k) * jnp.exp(d - m_t[..., None])
    num = w_inter[..., None] * jnp.einsum('bhtd,bhde->bhte', q, C) + jnp.einsum('bhts,bhse->bhte', s, v)
    den = w_inter * jnp.einsum('bhtd,bhd->bht', q, n) + jnp.sum(s, axis=-1)
    h = num / jnp.maximum(jnp.abs(den), jnp.exp(-m_t))[..., None]
    m_new = m_t[..., -1]
    w_s = jnp.exp(b[..., -1:] - b + logi - m_new[..., None])
    decay = jnp.exp(b[..., -1] + m - m_new)
    C_new = decay[..., None, None] * C + jnp.einsum('bhsd,bhse->bhde', k * w_s[..., None], v)
    n_new = decay[..., None] * n + jnp.einsum('bhs,bhsd->bhd', w_s, k)
    return h, C_new, n_new, m_new


def mlstm_prompt(state, q, k, v, logi, logf):
    B, H, L, _ = q.shape
    h_meta, C, n, m = mlstm_chunk(*state, q[:, :, :N_META], k[:, :, :N_META], v[:, :, :N_META],
                                  logi[:, :, :N_META], logf[:, :, :N_META])
    nc = (L - N_META) // CHUNK

    def to_chunks(a):
        a = a[:, :, N_META:]
        return jnp.moveaxis(a.reshape((B, H, nc, CHUNK) + a.shape[3:]), 2, 0)

    def step(carry, xs):
        h, C1, n1, m1 = mlstm_chunk(*carry, *xs)
        return (C1, n1, m1), h

    (C, n, m), hs = lax.scan(step, (C, n, m), tuple(to_chunks(a) for a in (q, k, v, logi, logf)))
    hs = jnp.moveaxis(hs, 0, 2).reshape(B, H, L - N_META, DH_C)
    return jnp.concatenate([h_meta, hs], axis=2), C, n, m


def mlstm_sample(state, q, k, v, logi, logf):
    return mlstm_chunk(*state, q, k, v, logi, logf)


def head_layernorm(h, g):
    mu = jnp.mean(h, axis=-1, keepdims=True)
    var = jnp.mean(jnp.square(h - mu), axis=-1, keepdims=True)
    hn = (h - mu) * lax.rsqrt(var + EPS)
    B, H, T, dh = h.shape
    return jnp.transpose(hn, (0, 2, 1, 3)).reshape(B, T, H * dh) * g.astype(F32)


def odd_mixer(xn, conv_prefix, state, scan_fn, w_in, b_gate, conv_w, conv_b, w_q, w_k, w_v, skip, on_g, w_out):
    B, T, _ = xn.shape
    proj = xn @ w_in
    xm = proj[..., :INNER_C]
    z = proj[..., INNER_C:2 * INNER_C].astype(F32)
    gates = (proj[..., 2 * INNER_C:] + b_gate).astype(F32)
    logi = jnp.transpose(gates[..., :H_C], (0, 2, 1))
    logf = jnp.transpose(jax.nn.log_sigmoid(gates[..., H_C:]), (0, 2, 1))
    xm_ext = jnp.concatenate([conv_prefix.astype(xm.dtype), xm], axis=1)
    conv = conv_b.astype(F32)
    for j in range(CONV_W):
        conv = conv + xm_ext[:, j:j + T].astype(F32) * conv_w[j].astype(F32)
    ca = jax.nn.silu(conv)
    ca_h = ca.reshape(B, T, H_C, DH_C)
    q = jnp.einsum('bthd,hde->bhte', ca_h, w_q.astype(F32))
    k = jnp.einsum('bthd,hde->bhte', ca_h, w_k.astype(F32)) * (DH_C ** -0.5)
    v = jnp.einsum('bthd,hde->bhte', xm.astype(F32).reshape(B, T, H_C, DH_C), w_v.astype(F32))
    h, C, n, m = scan_fn(state, q, k, v, logi, logf)
    out = (head_layernorm(h, on_g) + skip.astype(F32) * ca) * jax.nn.silu(z)
    return out.astype(xn.dtype) @ w_out, C, n, m, xm_ext[:, -(CONV_W - 1):]


def setup_inputs(seed: int = 0) -> dict:
    key = jax.random.key(seed)
    ks = jax.random.split(key, 32)
    n_pages = PAST_LEN // PAGE_SIZE
    n_used = DEC_BATCH * n_pages
    n_pool = n_used + max(1, n_used // 4)
    perm = jax.random.permutation(ks[0], n_pool)
    page_table = perm[:n_used].reshape(DEC_BATCH, n_pages).astype(jnp.int32)

    def nrm(k, shape, s):
        return jax.random.normal(k, shape, F32) * s

    b_i = nrm(ks[21], (N_C_LAYERS, H_C), 0.1)
    b_f = jnp.linspace(3.0, 6.0, H_C, dtype=F32)[None, :] + nrm(ks[22], (N_C_LAYERS, H_C), 0.1)
    return {
        'x_prompt': nrm(ks[1], (BATCH, SEQ, D_MODEL), 1.0),
        'x_sample': nrm(ks[2], (DEC_BATCH, DEC_SEQ, D_MODEL), 1.0),
        'cache_sb_k': nrm(ks[3], (N_A_LAYERS, n_pool, PAGE_SIZE, H_A, DH_A), 1.0),
        'cache_sb_v': nrm(ks[4], (N_A_LAYERS, n_pool, PAGE_SIZE, H_A, DH_A), 1.0),
        'page_table': page_table,
        'state_pool': nrm(ks[5], (N_A_LAYERS, DEC_BATCH, POOL_MAX - 1, W_B), 1.0),
        'state_C': nrm(ks[6], (N_C_LAYERS, DEC_BATCH, H_C, DH_C, DH_C), 0.02),
        'state_n': nrm(ks[7], (N_C_LAYERS, DEC_BATCH, H_C, DH_C), 0.1),
        'state_m': nrm(ks[8], (N_C_LAYERS, DEC_BATCH, H_C), 1.0),
        'state_conv': nrm(ks[9], (N_C_LAYERS, DEC_BATCH, CONV_W - 1, INNER_C), 1.0),
        'meta_tokens': nrm(ks[10], (N_META, D_MODEL), 1.0),
        'norm_g': 1.0 + nrm(ks[11], (DEPTH, D_MODEL), 0.02),
        'final_norm_g': 1.0 + nrm(ks[12], (D_MODEL,), 0.02),
        'w_in_a': nrm(ks[13], (N_A_LAYERS, D_MODEL, 4 * W_A + 2 * W_B), D_MODEL ** -0.5),
        'w_out_a': nrm(ks[14], (N_A_LAYERS, W_A + W_B, D_MODEL), (W_A + W_B) ** -0.5),
        'sb_bias': SB_BIAS_INIT + nrm(ks[28], (N_A_LAYERS, H_A), 0.1),
        'w_pool': nrm(ks[15], (N_A_LAYERS, POOL_GROUPS, POOL_GC, POOL_GC), POOL_GC ** -0.5),
        'pool_scale': 1.0 + nrm(ks[16], (N_A_LAYERS, W_B), 0.1),
        'w_in_c': nrm(ks[17], (N_C_LAYERS, D_MODEL, 2 * INNER_C + 2 * H_C), D_MODEL ** -0.5),
        'b_gate_c': jnp.concatenate([b_i, b_f], axis=-1),
        'conv_w': nrm(ks[18], (N_C_LAYERS, CONV_W, INNER_C), CONV_W ** -0.5),
        'conv_b': nrm(ks[19], (N_C_LAYERS, INNER_C), 0.02),
        'w_q': nrm(ks[20], (N_C_LAYERS, H_C, DH_C, DH_C), DH_C ** -0.5),
        'w_k': nrm(ks[23], (N_C_LAYERS, H_C, DH_C, DH_C), DH_C ** -0.5),
        'w_v': nrm(ks[24], (N_C_LAYERS, H_C, DH_C, DH_C), DH_C ** -0.5),
        'skip_c': 1.0 + nrm(ks[25], (N_C_LAYERS, INNER_C), 0.1),
        'outnorm_g': 1.0 + nrm(ks[26], (N_C_LAYERS, INNER_C), 0.02),
        'w_out_c': nrm(ks[27], (N_C_LAYERS, INNER_C, D_MODEL), INNER_C ** -0.5),
    }


def reference(x_prompt, x_sample, cache_sb_k, cache_sb_v, page_table, state_pool, state_C, state_n, state_m,
              state_conv, meta_tokens, norm_g, final_norm_g, w_in_a, w_out_a, sb_bias, w_pool, pool_scale, w_in_c,
              b_gate_c, conv_w, conv_b, w_q, w_k, w_v, skip_c, outnorm_g, w_out_c):
    Bp = x_prompt.shape[0]
    Bs, Ts, _ = x_sample.shape
    past = page_table.shape[1] * cache_sb_k.shape[2]
    meta = jnp.broadcast_to(meta_tokens.astype(x_prompt.dtype)[None], (Bp, N_META, D_MODEL))
    h_p = jnp.concatenate([meta, x_prompt], axis=1)
    h_s = x_sample
    pos_p = jnp.arange(h_p.shape[1])
    pos_s = past + jnp.arange(Ts)
    kpos_s = jnp.arange(past + Ts)
    sbk_p, sbv_p, sbk_s, sbv_s, pool_p, pool_s = [], [], [], [], [], []
    C_p, C_s, n_p, n_s, m_p, m_s, cv_p, cv_s = [], [], [], [], [], [], [], []
    for layer in range(DEPTH):
        if layer % 2 == 0:
            ie = layer // 2
            wa = (w_in_a[ie], w_out_a[ie], sb_bias[ie], w_pool[ie], pool_scale[ie])
            zeros_pool = jnp.zeros((Bp, POOL_MAX - 1, W_B), h_p.dtype)
            y, k, v, pst = even_mixer(rmsnorm(h_p, norm_g[layer]), pos_p, zeros_pool, sb_prompt, *wa)
            h_p = h_p + y
            sbk_p.append(k)
            sbv_p.append(v)
            pool_p.append(pst)
            k_past = cache_sb_k[ie][page_table].reshape(Bs, past, H_A, DH_A)
            v_past = cache_sb_v[ie][page_table].reshape(Bs, past, H_A, DH_A)

            def sb_sample(q, k, v, bias, k_past=k_past, v_past=v_past):
                k_all = jnp.concatenate([k_past.astype(k.dtype), k], axis=1)
                v_all = jnp.concatenate([v_past.astype(v.dtype), v], axis=1)
                return sb_block(q, k_all, v_all, bias, pos_s, kpos_s)

            y, k, v, pst = even_mixer(rmsnorm(h_s, norm_g[layer]), pos_s, state_pool[ie], sb_sample, *wa)
            h_s = h_s + y
            sbk_s.append(k)
            sbv_s.append(v)
            pool_s.append(pst)
        else:
            io = layer // 2
            wc = (w_in_c[io], b_gate_c[io], conv_w[io], conv_b[io], w_q[io], w_k[io], w_v[io],
                  skip_c[io], outnorm_g[io], w_out_c[io])
            init = (jnp.zeros((Bp, H_C, DH_C, DH_C), F32), jnp.zeros((Bp, H_C, DH_C), F32),
                    jnp.zeros((Bp, H_C), F32))
            zeros_conv = jnp.zeros((Bp, CONV_W - 1, INNER_C), h_p.dtype)
            y, C, n, m, cv = odd_mixer(rmsnorm(h_p, norm_g[layer]), zeros_conv, init, mlstm_prompt, *wc)
            h_p = h_p + y
            C_p.append(C)
            n_p.append(n)
            m_p.append(m)
            cv_p.append(cv)
            st = (state_C[io], state_n[io], state_m[io])
            y, C, n, m, cv = odd_mixer(rmsnorm(h_s, norm_g[layer]), state_conv[io], st, mlstm_sample, *wc)
            h_s = h_s + y
            C_s.append(C)
            n_s.append(n)
            m_s.append(m)
            cv_s.append(cv)
    y_prompt = rmsnorm(h_p, final_norm_g)[:, N_META:]
    y_sample = rmsnorm(h_s, final_norm_g)
    return (y_prompt, y_sample,
            jnp.stack(sbk_p), jnp.stack(sbv_p), jnp.stack(sbk_s), jnp.stack(sbv_s),
            jnp.stack(pool_p), jnp.stack(pool_s),
            jnp.stack(C_p), jnp.stack(C_s), jnp.stack(n_p), jnp.stack(n_s),
            jnp.stack(m_p), jnp.stack(m_s), jnp.stack(cv_p), jnp.stack(cv_s))
```

```python
import functools

import jax
import jax.numpy as jnp
from jax import lax
from jax.experimental import pallas as pl
from jax.experimental.pallas import tpu as pltpu

F32 = jnp.float32
BF16 = jnp.bfloat16

D_MODEL = 1024
N_META = 16
W_A = 512
H_A = 8
DH_A = 64
W_B = 512
POOL_WINDOWS = (2, 4, 8, 16)
POOL_GC = 128
POOL_MAX = 16
INNER_C = 2048
H_C = 4
DH_C = 512
CONV_W = 4
EPS = 1e-6

TILE = 128
PAD = TILE - N_META
NEG = -1e30
VMEM_LIMIT = 56 * 1024 * 1024


def _cparams(*sem):
    return pltpu.CompilerParams(dimension_semantics=sem, vmem_limit_bytes=VMEM_LIMIT)


def _silu(x):
    return x / (1.0 + jnp.exp(-x))


def _softplus(z):
    return jnp.maximum(z, 0.0) + jnp.log1p(jnp.exp(-jnp.abs(z)))


def _split2(x):
    hi = x.astype(BF16)
    lo = (x - hi.astype(F32)).astype(BF16)
    return hi, lo


def _split3(x):
    hi = x.astype(BF16)
    r = x - hi.astype(F32)
    mid = r.astype(BF16)
    lo = (r - mid.astype(F32)).astype(BF16)
    return hi, mid, lo


def _dot(a, b):
    return jnp.dot(a, b, preferred_element_type=F32)


def _dot_nt(a, b):
    return lax.dot_general(a, b, (((1,), (1,)), ((), ())), preferred_element_type=F32)


def _dot_tn(a, b):
    return lax.dot_general(a, b, (((0,), (0,)), ((), ())), preferred_element_type=F32)


def _full(shape):
    nd = len(shape)
    return pl.BlockSpec(shape, lambda *_: (0,) * nd)


def _norm_proj_kernel(x_ref, g_ref, w_ref, *out_refs, outs):
    x = x_ref[...]
    ms = jnp.mean(x * x, axis=-1, keepdims=True)
    xn = (x * lax.rsqrt(ms + EPS) * g_ref[...]).astype(BF16)
    for o_ref, (off, width, scale) in zip(out_refs, outs):
        acc = _dot(xn, w_ref[:, off:off + width])
        if scale != 1.0:
            acc = acc * scale
        o_ref[...] = acc.astype(o_ref.dtype)


def _norm_proj(x, g, w_bf, outs, tm):
    rows = x.shape[0]
    kern = functools.partial(_norm_proj_kernel, outs=[(o, w, s) for o, w, s, _ in outs])
    return pl.pallas_call(
        kern,
        grid=(rows // tm,),
        in_specs=[pl.BlockSpec((tm, D_MODEL), lambda i: (i, 0)), _full((1, D_MODEL)), _full(w_bf.shape)],
        out_specs=[pl.BlockSpec((tm, w), lambda i: (i, 0)) for _, w, _, _ in outs],
        out_shape=[jax.ShapeDtypeStruct((rows, w), dt) for _, w, _, dt in outs],
        compiler_params=_cparams("parallel"),
        name="norm_proj",
    )(x, g, w_bf)


def _norm_proj1_kernel(x_ref, g_ref, w_ref, wg_ref, bg_ref, xm_ref, z_ref, gate_ref):
    x = x_ref[...]
    ms = jnp.mean(x * x, axis=-1, keepdims=True)
    xn = (x * lax.rsqrt(ms + EPS) * g_ref[...]).astype(BF16)
    xm_ref[...] = _dot(xn, w_ref[:, :INNER_C])
    z_ref[...] = _dot(xn, w_ref[:, INNER_C:])
    gl = _dot(xn, wg_ref[...]) + bg_ref[...]
    lane = lax.broadcasted_iota(jnp.int32, gl.shape, 1) % TILE
    gate_ref[...] = jnp.where(lane == 1, -_softplus(-gl), gl)


def _norm_proj1(x, g, w_bf, wg_bf, bg, tm):
    rows = x.shape[0]
    return pl.pallas_call(
        _norm_proj1_kernel,
        grid=(rows // tm,),
        in_specs=[pl.BlockSpec((tm, D_MODEL), lambda i: (i, 0)), _full((1, D_MODEL)), _full(w_bf.shape),
                  _full(wg_bf.shape), _full(bg.shape)],
        out_specs=[pl.BlockSpec((tm, INNER_C), lambda i: (i, 0)), pl.BlockSpec((tm, INNER_C), lambda i: (i, 0)),
                   pl.BlockSpec((tm, H_C * TILE), lambda i: (i, 0))],
        out_shape=[jax.ShapeDtypeStruct((rows, INNER_C), F32), jax.ShapeDtypeStruct((rows, INNER_C), F32),
                   jax.ShapeDtypeStruct((rows, H_C * TILE), F32)],
        compiler_params=_cparams("parallel"),
        name="norm_proj1",
    )(x, g, w_bf, wg_bf, bg)


def _sb_block(z, mask, cb, uj):
    sp = _softplus(z)
    lneg = -sp if mask is None else jnp.where(mask, -sp, 0.0)
    hi, lo = _split2(lneg)
    t = _dot(jnp.concatenate([hi, lo], axis=1), uj)
    a = jnp.exp(z - sp + t[:, :TILE] + cb)
    if mask is not None:
        a = jnp.where(mask, a, 0.0)
    return a, cb + t[:, TILE:]


def _sb_prompt_kernel(bias_ref, q_ref, k_ref, v_ref, uj_ref, o_ref):
    qi = pl.program_id(1)
    row_q = qi * TILE + lax.broadcasted_iota(jnp.int32, (TILE, TILE), 0)
    col = lax.broadcasted_iota(jnp.int32, (TILE, TILE), 1)
    uj = uj_ref[...]
    for h in range(H_A):
        cs = slice(h * DH_A, (h + 1) * DH_A)
        qh = q_ref[:, cs]
        bias = bias_ref[h]

        def body(j, carry, qh=qh, bias=bias, cs=cs):
            acc, cb = carry
            ks = pl.multiple_of((qi - j) * TILE, TILE)
            kh = k_ref[pl.ds(ks, TILE), cs]
            vh = v_ref[pl.ds(ks, TILE), cs]
            z = _dot_nt(qh, kh) + bias
            col_k = ks + col
            mask = jnp.logical_and(col_k >= PAD, col_k < row_q)
            a, cb = _sb_block(z, mask, cb, uj)
            return acc + _dot(a.astype(BF16), vh), cb

        acc, _ = lax.fori_loop(0, qi + 1, body, (jnp.zeros((TILE, DH_A), F32), jnp.zeros((TILE, TILE), F32)))
        o_ref[:, cs] = acc


def _sb_prompt(bias, q_bf, k_bf, v_bf, uj, nb, lp):
    nq = lp // TILE
    return pl.pallas_call(
        _sb_prompt_kernel,
        grid=(nb, nq),
        in_specs=[pl.BlockSpec(memory_space=pltpu.SMEM),
                  pl.BlockSpec((TILE, W_A), lambda b, i: (b * nq + i, 0)),
                  pl.BlockSpec((lp, W_A), lambda b, i: (b, 0)),
                  pl.BlockSpec((lp, W_A), lambda b, i: (b, 0)),
                  _full(uj.shape)],
        out_specs=pl.BlockSpec((TILE, W_A), lambda b, i: (b * nq + i, 0)),
        out_shape=jax.ShapeDtypeStruct((nb * lp, W_A), F32),
        compiler_params=_cparams("parallel", "arbitrary"),
        name="sb_prompt",
    )(bias, q_bf, k_bf, v_bf, uj)


def _sb_sample_kernel(pt_ref, q_ref, kn_ref, vn_ref, hm_ref, bias_ref, uj_ref, *rest, n_pages, ts):
    kp_refs = rest[:n_pages]
    vp_refs = rest[n_pages:2 * n_pages]
    o_ref = rest[2 * n_pages]
    hm = hm_ref[...]
    qbd = q_ref[...] * hm
    bias = bias_ref[...]
    uj = uj_ref[...]
    rows = ts * H_A
    t_of_row = lax.broadcasted_iota(jnp.int32, (rows, 1), 0) // H_A
    out = jnp.zeros((rows, W_A), F32)
    carry = jnp.zeros((rows, 1), F32)
    for j in range(ts - 1, -1, -1):
        z = jnp.sum(qbd * kn_ref[j:j + 1, :], axis=-1, keepdims=True) + bias[:, 0:1]
        sp = _softplus(z)
        m = j < t_of_row
        a = jnp.where(m, jnp.exp(z - sp + carry), 0.0)
        carry = carry + jnp.where(m, -sp, 0.0)
        out = out + a * vn_ref[j:j + 1, :]
    cb = jnp.broadcast_to(carry, (rows, TILE))
    qbd_bf = qbd.astype(BF16)
    for p in range(n_pages - 1, -1, -1):
        z = _dot_nt(qbd_bf, kp_refs[p][...].astype(BF16)) + bias
        a, cb = _sb_block(z, None, cb, uj)
        out = out + _dot(a.astype(BF16), vp_refs[p][...].astype(BF16))
    o_ref[...] = jnp.sum((out * hm).reshape(ts, H_A, W_A), axis=1)


def _sb_sample(page_table, q_rep, kn, vn, hm, bias_rows, uj, cache_k, cache_v):
    bs, n_pages = page_table.shape
    ts = kn.shape[1]
    page = cache_k.shape[1]
    rows = ts * H_A

    def page_spec(p):
        return pl.BlockSpec((None, page, W_A), lambda s, pt: (pt[s * n_pages + p], 0, 0))

    grid_spec = pltpu.PrefetchScalarGridSpec(
        num_scalar_prefetch=1,
        grid=(bs,),
        in_specs=[pl.BlockSpec((None, rows, W_A), lambda s, pt: (s, 0, 0)),
                  pl.BlockSpec((None, ts, W_A), lambda s, pt: (s, 0, 0)),
                  pl.BlockSpec((None, ts, W_A), lambda s, pt: (s, 0, 0)),
                  pl.BlockSpec(hm.shape, lambda s, pt: (0, 0)),
                  pl.BlockSpec(bias_rows.shape, lambda s, pt: (0, 0)),
                  pl.BlockSpec(uj.shape, lambda s, pt: (0, 0))]
                 + [page_spec(p) for p in range(n_pages)] * 2,
        out_specs=pl.BlockSpec((None, ts, W_A), lambda s, pt: (s, 0, 0)),
    )
    return pl.pallas_call(
        functools.partial(_sb_sample_kernel, n_pages=n_pages, ts=ts),
        grid_spec=grid_spec,
        out_shape=jax.ShapeDtypeStruct((bs, ts, W_A), F32),
        compiler_params=_cparams("arbitrary"),
        name="sb_sample",
    )(page_table.reshape(-1), q_rep, kn, vn, hm, bias_rows, uj,
      *([cache_k] * n_pages), *([cache_v] * n_pages))


def _tail0_math(o_a, g_a, u, win_sums, cnts, g_b, h, wp_ref, ps, wo_ref):
    parts = [(o_a * _silu(g_a)).astype(BF16)]
    for g in range(len(POOL_WINDOWS)):
        cs = slice(g * POOL_GC, (g + 1) * POOL_GC)
        diff = win_sums[g] / cnts[g] - u[:, cs]
        y = _dot(diff.astype(BF16), wp_ref[g]) * ps[:, cs]
        parts.append((y * _silu(g_b[:, cs])).astype(BF16))
    mixed = jnp.concatenate(parts, axis=1)
    return h + _dot(mixed, wo_ref[...])


def _tail0_prompt_kernel(oa_ref, ga_ref, u_ref, halo_ref, gb_ref, h_ref, wp_ref, ps_ref, wo_ref, out_ref,
                         ext_ref, *, tm, lp):
    i = pl.program_id(0)
    ext_ref[0:POOL_MAX, :] = halo_ref[...]
    ext_ref[POOL_MAX:POOL_MAX + tm, :] = u_ref[...]
    row = (i * tm) % lp + lax.broadcasted_iota(jnp.int32, (tm, 1), 0)
    pos = jnp.where(row >= lp, row - lp, row) - PAD
    win_sums, cnts = [], []
    for g, w in enumerate(POOL_WINDOWS):
        cs = slice(g * POOL_GC, (g + 1) * POOL_GC)
        s = ext_ref[POOL_MAX:POOL_MAX + tm, cs]
        for j in range(1, w):
            s = s + ext_ref[POOL_MAX - j:POOL_MAX - j + tm, cs]
        win_sums.append(s)
        cnts.append(jnp.clip(pos + 1, 1, w).astype(F32))
    out_ref[...] = _tail0_math(oa_ref[...], ga_ref[...], u_ref[...], win_sums, cnts, gb_ref[...], h_ref[...],
                               wp_ref, ps_ref[...], wo_ref)


def _tail0_prompt(o_a, g_a, u, g_b, h, wp_bf, ps, wo_bf, tm, lp):
    rows = h.shape[0]
    hb = tm // POOL_MAX
    row = lambda i: (i, 0)
    return pl.pallas_call(
        functools.partial(_tail0_prompt_kernel, tm=tm, lp=lp),
        grid=(rows // tm,),
        in_specs=[pl.BlockSpec((tm, W_A), row), pl.BlockSpec((tm, W_A), row), pl.BlockSpec((tm, W_B), row),
                  pl.BlockSpec((POOL_MAX, W_B), lambda i: (jnp.maximum(i * hb - 1, 0), 0)),
                  pl.BlockSpec((tm, W_B), row), pl.BlockSpec((tm, D_MODEL), row),
                  _full(wp_bf.shape), _full(ps.shape), _full(wo_bf.shape)],
        out_specs=pl.BlockSpec((tm, D_MODEL), row),
        out_shape=jax.ShapeDtypeStruct((rows, D_MODEL), F32),
        scratch_shapes=[pltpu.VMEM((POOL_MAX + tm, W_B), F32)],
        compiler_params=_cparams("parallel"),
        name="tail0_prompt",
    )(o_a, g_a, u, u, g_b, h, wp_bf, ps, wo_bf)


def _tail0_sample_kernel(oa_ref, ga_ref, ext_ref, gb_ref, h_ref, wp_ref, ps_ref, wo_ref, out_ref, *, ts, bs):
    pre = POOL_MAX - 1
    u = jnp.concatenate([ext_ref[pre + t] for t in range(ts)], axis=0)
    win_sums, cnts = [], []
    for g, w in enumerate(POOL_WINDOWS):
        cs = slice(g * POOL_GC, (g + 1) * POOL_GC)
        per_t = []
        for t in range(ts):
            s = ext_ref[pre + t, :, cs]
            for j in range(1, w):
                s = s + ext_ref[pre + t - j, :, cs]
            per_t.append(s)
        win_sums.append(jnp.concatenate(per_t, axis=0))
        cnts.append(float(w))
    out_ref[...] = _tail0_math(oa_ref[...], ga_ref[...], u, win_sums, cnts, gb_ref[...], h_ref[...],
                               wp_ref, ps_ref[...], wo_ref)


def _tail0_sample(o_a, g_a, ext_t, g_b, h, wp_bf, ps, wo_bf):
    rows = h.shape[0]
    ts = ext_t.shape[0] - (POOL_MAX - 1)
    args = (o_a, g_a, ext_t, g_b, h, wp_bf, ps, wo_bf)
    return pl.pallas_call(
        functools.partial(_tail0_sample_kernel, ts=ts, bs=ext_t.shape[1]),
        grid=(1,),
        in_specs=[_full(a.shape) for a in args],
        out_specs=_full((rows, D_MODEL)),
        out_shape=jax.ShapeDtypeStruct((rows, D_MODEL), F32),
        compiler_params=_cparams("arbitrary"),
        name="tail0_sample",
    )(*args)


def _qkv_math(ca, xm, wq_ref, wk_ref, wv_ref, q_ref, k_ref, v_ref):
    ca_bf = ca.astype(BF16)
    xm_bf = xm.astype(BF16)
    for h in range(H_C):
        cs = slice(h * DH_C, (h + 1) * DH_C)
        q_ref[:, cs] = _dot(ca_bf[:, cs], wq_ref[h]).astype(BF16)
        k_ref[:, cs] = (_dot(ca_bf[:, cs], wk_ref[h]) * (DH_C ** -0.5)).astype(BF16)
        v_ref[:, cs] = _dot(xm_bf[:, cs], wv_ref[h]).astype(BF16)


def _conv_qkv_prompt_kernel(xm_ref, halo_ref, cw_ref, cb_ref, wq_ref, wk_ref, wv_ref,
                            ca_ref, q_ref, k_ref, v_ref, ext_ref, *, tm):
    ext_ref[0:8, :] = halo_ref[...]
    ext_ref[8:8 + tm, :] = xm_ref[...]
    conv = jnp.broadcast_to(cb_ref[...], (tm, INNER_C))
    for j in range(CONV_W):
        s = 8 - (CONV_W - 1) + j
        conv = conv + ext_ref[s:s + tm, :] * cw_ref[j:j + 1, :]
    ca = _silu(conv)
    ca_ref[...] = ca
    _qkv_math(ca, xm_ref[...], wq_ref, wk_ref, wv_ref, q_ref, k_ref, v_ref)


def _conv_qkv_prompt(xm, cw, cb, wq_bf, wk_bf, wv_bf, tm):
    rows = xm.shape[0]
    row = lambda i: (i, 0)
    return pl.pallas_call(
        functools.partial(_conv_qkv_prompt_kernel, tm=tm),
        grid=(rows // tm,),
        in_specs=[pl.BlockSpec((tm, INNER_C), row),
                  pl.BlockSpec((8, INNER_C), lambda i: (jnp.maximum(i * (tm // 8) - 1, 0), 0)),
                  _full(cw.shape), _full(cb.shape), _full(wq_bf.shape), _full(wk_bf.shape), _full(wv_bf.shape)],
        out_specs=[pl.BlockSpec((tm, INNER_C), row)] * 4,
        out_shape=[jax.ShapeDtypeStruct((rows, INNER_C), F32)] + [jax.ShapeDtypeStruct((rows, INNER_C), BF16)] * 3,
        scratch_shapes=[pltpu.VMEM((8 + tm, INNER_C), F32)],
        compiler_params=_cparams("parallel"),
        name="conv_qkv_prompt",
    )(xm, xm, cw, cb, wq_bf, wk_bf, wv_bf)


def _conv_qkv_sample_kernel(ext_ref, cw_ref, cb_ref, wq_ref, wk_ref, wv_ref, ca_ref, q_ref, k_ref, v_ref):
    t = pl.program_id(0)
    bs = ext_ref.shape[1]
    conv = jnp.broadcast_to(cb_ref[...], (bs, INNER_C))
    for j in range(CONV_W):
        conv = conv + ext_ref[t + j] * cw_ref[j:j + 1, :]
    ca = _silu(conv)
    ca_ref[...] = ca
    _qkv_math(ca, ext_ref[t + CONV_W - 1], wq_ref, wk_ref, wv_ref, q_ref, k_ref, v_ref)


def _conv_qkv_sample(ext_t, cw, cb, wq_bf, wk_bf, wv_bf):
    ts = ext_t.shape[0] - (CONV_W - 1)
    bs = ext_t.shape[1]
    row = lambda t: (t, 0)
    return pl.pallas_call(
        _conv_qkv_sample_kernel,
        grid=(ts,),
        in_specs=[_full(ext_t.shape), _full(cw.shape), _full(cb.shape), _full(wq_bf.shape), _full(wk_bf.shape),
                  _full(wv_bf.shape)],
        out_specs=[pl.BlockSpec((bs, INNER_C), row)] * 4,
        out_shape=[jax.ShapeDtypeStruct((ts * bs, INNER_C), F32)]
                  + [jax.ShapeDtypeStruct((ts * bs, INNER_C), BF16)] * 3,
        compiler_params=_cparams("arbitrary"),
        name="conv_qkv_sample",
    )(ext_t, cw, cb, wq_bf, wk_bf, wv_bf)


def _mlstm_chunk(q, k, v, g, valid, C, n, m, lower, sel):
    T = q.shape[0]
    lane = lax.broadcasted_iota(jnp.int32, g.shape, 1)
    g = jnp.where(valid, g, jnp.where(lane == 0, NEG, 0.0))
    g_parts = _split3(g)
    cum = sum(_dot(lower, p) for p in g_parts)
    rows_g = sum(_dot_nt(sel, p) for p in g_parts)
    rows_c = sum(_dot_nt(sel, p) for p in _split3(cum))
    logi_c, b_c = g[:, 0:1], cum[:, 1:2]
    logi_r, b_r = rows_g[0:1, :], rows_c[1:2, :]
    tri = lax.broadcasted_iota(jnp.int32, (T, T), 1) <= lax.broadcasted_iota(jnp.int32, (T, T), 0)
    d = jnp.where(tri, b_c - b_r + logi_r, NEG)
    m_inter = b_c + m
    m_t = jnp.maximum(m_inter, jnp.max(d, axis=-1, keepdims=True))
    w_inter = jnp.exp(m_inter - m_t)
    s = jnp.where(tri, _dot_nt(q, k) * jnp.exp(d - m_t), 0.0)
    qf = q.astype(F32)
    num = w_inter * _dot(q, C.astype(BF16)) + _dot(s.astype(BF16), v)
    den = w_inter * jnp.sum(qf * n, axis=-1, keepdims=True) + jnp.sum(s, axis=-1, keepdims=True)
    h = num / jnp.maximum(jnp.abs(den), jnp.exp(-m_t))
    m_new = m_t[T - 1:T, :]
    b_last = b_c[T - 1:T, :]
    w_s = jnp.exp(b_last - b_c + logi_c - m_new)
    decay = jnp.exp(b_last + m - m_new)
    kw = k.astype(F32) * w_s
    C_new = decay * C + _dot_tn(kw.astype(BF16), v)
    n_new = decay * n + jnp.sum(kw, axis=0, keepdims=True)
    mu = jnp.mean(h, axis=-1, keepdims=True)
    hc = h - mu
    var = jnp.mean(hc * hc, axis=-1, keepdims=True)
    return hc * lax.rsqrt(var + EPS), C_new, n_new, m_new


def _mlstm_prompt_kernel(q_ref, k_ref, v_ref, g_ref, lower_ref, sel_ref, hn_ref, c_ref, n_ref, m_ref):
    c = pl.program_id(2)

    @pl.when(c == 0)
    def _():
        c_ref[...] = jnp.zeros_like(c_ref)
        n_ref[...] = jnp.zeros_like(n_ref)
        m_ref[...] = jnp.zeros_like(m_ref)

    valid = c * TILE + lax.broadcasted_iota(jnp.int32, (TILE, 1), 0) >= PAD
    hn, C, n, m = _mlstm_chunk(q_ref[...], k_ref[...], v_ref[...], g_ref[...], valid,
                               c_ref[...], n_ref[...], m_ref[:, 0:1], lower_ref[...], sel_ref[...])
    hn_ref[...] = hn
    c_ref[...] = C
    n_ref[...] = n
    m_ref[...] = jnp.broadcast_to(m, m_ref.shape)


def _mlstm_prompt(q, k, v, g, lower, sel, nb, lp):
    nc = lp // TILE
    blk = lambda b, h, c: (b * nc + c, h)
    st = lambda b, h, c: (b * H_C + h, 0, 0)
    return pl.pallas_call(
        _mlstm_prompt_kernel,
        grid=(nb, H_C, nc),
        in_specs=[pl.BlockSpec((TILE, DH_C), blk)] * 3
                 + [pl.BlockSpec((TILE, TILE), blk), _full(lower.shape), _full(sel.shape)],
        out_specs=[pl.BlockSpec((TILE, DH_C), blk), pl.BlockSpec((None, DH_C, DH_C), st),
                   pl.BlockSpec((None, 1, DH_C), st), pl.BlockSpec((None, 1, TILE), st)],
        out_shape=[jax.ShapeDtypeStruct((nb * lp, INNER_C), F32),
                   jax.ShapeDtypeStruct((nb * H_C, DH_C, DH_C), F32),
                   jax.ShapeDtypeStruct((nb * H_C, 1, DH_C), F32),
                   jax.ShapeDtypeStruct((nb * H_C, 1, TILE), F32)],
        compiler_params=_cparams("parallel", "parallel", "arbitrary"),
        name="mlstm_prompt",
    )(q, k, v, g, lower, sel)


def _mlstm_sample_kernel(q_ref, k_ref, v_ref, g_ref, c_in, n_in, m_in, lower_ref, sel_ref,
                         hn_ref, c_ref, n_ref, m_ref, *, ts):
    rp = q_ref.shape[0]

    def pad(x):
        return jnp.concatenate([x, jnp.zeros((TILE - rp, x.shape[1]), x.dtype)], axis=0)

    valid = lax.broadcasted_iota(jnp.int32, (TILE, 1), 0) < ts
    hn, C, n, m = _mlstm_chunk(pad(q_ref[...]), pad(k_ref[...]), pad(v_ref[...]), pad(g_ref[...]), valid,
                               c_in[...], n_in[...], m_in[:, 0:1], lower_ref[...], sel_ref[...])
    hn_ref[...] = hn[:rp]
    c_ref[...] = C
    n_ref[...] = n
    m_ref[...] = jnp.broadcast_to(m, m_ref.shape)


def _mlstm_sample(q, k, v, g, c0, n0, m0, lower, sel, ts):
    bs, rp = q.shape[0], q.shape[1]
    blk = lambda s, h: (s, 0, h)
    st = lambda s, h: (s * H_C + h, 0, 0)
    return pl.pallas_call(
        functools.partial(_mlstm_sample_kernel, ts=ts),
        grid=(bs, H_C),
        in_specs=[pl.BlockSpec((None, rp, DH_C), blk)] * 3
                 + [pl.BlockSpec((None, rp, TILE), blk), pl.BlockSpec((None, DH_C, DH_C), st),
                    pl.BlockSpec((None, 1, DH_C), st), pl.BlockSpec((None, 1, TILE), st),
                    pl.BlockSpec(lower.shape, lambda s, h: (0, 0)), pl.BlockSpec(sel.shape, lambda s, h: (0, 0))],
        out_specs=[pl.BlockSpec((None, rp, DH_C), blk), pl.BlockSpec((None, DH_C, DH_C), st),
                   pl.BlockSpec((None, 1, DH_C), st), pl.BlockSpec((None, 1, TILE), st)],
        out_shape=[jax.ShapeDtypeStruct((bs, rp, INNER_C), F32),
                   jax.ShapeDtypeStruct((bs * H_C, DH_C, DH_C), F32),
                   jax.ShapeDtypeStruct((bs * H_C, 1, DH_C), F32),
                   jax.ShapeDtypeStruct((bs * H_C, 1, TILE), F32)],
        compiler_params=_cparams("parallel", "arbitrary"),
        name="mlstm_sample",
    )(q, k, v, g, c0, n0, m0, lower, sel)


def _tail1_kernel(hn_ref, ca_ref, z_ref, h_ref, og_ref, sk_ref, wo_ref, fg_ref, y_ref):
    mix = (hn_ref[...] * og_ref[...] + sk_ref[...] * ca_ref[...]) * _silu(z_ref[...])
    h2 = h_ref[...] + _dot(mix.astype(BF16), wo_ref[...])
    ms = jnp.mean(h2 * h2, axis=-1, keepdims=True)
    y_ref[...] = h2 * lax.rsqrt(ms + EPS) * fg_ref[...]


def _tail1(hn, ca, z, h, og, sk, wo_bf, fg, grid, in_map, out_map, out_rows, tm):
    return pl.pallas_call(
        _tail1_kernel,
        grid=grid,
        in_specs=[pl.BlockSpec((tm, INNER_C), in_map)] * 3 + [pl.BlockSpec((tm, D_MODEL), in_map)]
                 + [_full(og.shape), _full(sk.shape), _full(wo_bf.shape), _full(fg.shape)],
        out_specs=pl.BlockSpec((tm, D_MODEL), out_map),
        out_shape=jax.ShapeDtypeStruct((out_rows, D_MODEL), F32),
        compiler_params=_cparams(*(["parallel"] * len(grid))),
        name="tail1",
    )(hn, ca, z, h, og, sk, wo_bf, fg)


def kernel(x_prompt, x_sample, cache_sb_k, cache_sb_v, page_table, state_pool, state_C, state_n, state_m, state_conv, meta_tokens, norm_g, final_norm_g, w_in_a, w_out_a, sb_bias, w_pool, pool_scale, w_in_c, b_gate_c, conv_w, conv_b, w_q, w_k, w_v, skip_c, outnorm_g, w_out_c):
    nb, seq, _ = x_prompt.shape
    bs, ts, _ = x_sample.shape
    lp = PAD + N_META + seq
    n_pool, page = cache_sb_k.shape[1], cache_sb_k.shape[2]
    tm = 2 * TILE

    r = jnp.arange(TILE)
    strict = (r[:, None] > r[None, :]).astype(BF16)
    uj1 = jnp.concatenate([strict, jnp.ones((TILE, TILE), BF16)], axis=1)
    uj = jnp.concatenate([uj1, uj1], axis=0)
    lower = (r[None, :] <= r[:, None]).astype(BF16)
    sel = (jnp.arange(16)[:, None] == r[None, :]).astype(BF16)

    w_in_a_bf = w_in_a[0].astype(BF16)
    w_out_a_bf = w_out_a[0].astype(BF16)
    w_pool_bf = w_pool[0].astype(BF16)
    ps = pool_scale[0].reshape(1, W_B)
    g0 = norm_g[0].reshape(1, D_MODEL)
    outs0 = [(0, W_A, DH_A ** -0.5, BF16),
             (W_A, W_A, 1.0, F32), (2 * W_A, W_A, 1.0, F32),
             (W_A, W_A, 1.0, BF16), (2 * W_A, W_A, 1.0, BF16),
             (3 * W_A, W_A, 1.0, F32),
             (4 * W_A, W_B, 1.0, F32), (4 * W_A + W_B, W_B, 1.0, F32)]

    meta = jnp.broadcast_to(meta_tokens.astype(x_prompt.dtype)[None], (nb, N_META, D_MODEL))
    h_p = jnp.concatenate([jnp.zeros((nb, PAD, D_MODEL), x_prompt.dtype), meta, x_prompt], axis=1)
    h_p = h_p.reshape(nb * lp, D_MODEL)
    q_bf, k_p, v_p, k_bf, v_bf, ga_p, u_p, gb_p = _norm_proj(h_p, g0, w_in_a_bf, outs0, tm)
    oa_p = _sb_prompt(sb_bias[0], q_bf, k_bf, v_bf, uj, nb, lp)
    h1_p = _tail0_prompt(oa_p, ga_p, u_p, gb_p, h_p, w_pool_bf, ps, w_out_a_bf, tm, lp)

    def to_tm(a):
        return a.transpose(1, 0, 2).reshape(ts * bs, a.shape[-1])

    def to_sm(a):
        return a.reshape(ts, bs, a.shape[-1]).transpose(1, 0, 2)

    h_s = to_tm(x_sample)
    outs0_s = [(0, W_A, DH_A ** -0.5, F32)] + outs0[1:3] + outs0[5:]
    q_s, k_s, v_s, ga_s, u_s, gb_s = _norm_proj(h_s, g0, w_in_a_bf, outs0_s, tm)
    q_rep = jnp.repeat(to_sm(q_s), H_A, axis=1)
    row_head = jnp.arange(ts * H_A) % H_A
    hm = (row_head[:, None] == (jnp.arange(W_A) // DH_A)[None, :]).astype(F32)
    bias_rows = jnp.broadcast_to(sb_bias[0][row_head][:, None], (ts * H_A, TILE))
    oa_s = _sb_sample(page_table, q_rep, to_sm(k_s), to_sm(v_s), hm, bias_rows, uj,
                      cache_sb_k[0].reshape(n_pool, page, W_A), cache_sb_v[0].reshape(n_pool, page, W_A))
    pool_ext = jnp.concatenate([state_pool[0].transpose(1, 0, 2), u_s.reshape(ts, bs, W_B)], axis=0)
    h1_s = _tail0_sample(to_tm(oa_s), ga_s, pool_ext, gb_s, h_s, w_pool_bf, ps, w_out_a_bf)

    g1 = norm_g[1].reshape(1, D_MODEL)
    w_in_c_bf = w_in_c[0][:, :2 * INNER_C].astype(BF16)
    wg = jnp.zeros((D_MODEL, H_C, TILE), F32)
    wg = wg.at[:, :, 0].set(w_in_c[0][:, 2 * INNER_C:2 * INNER_C + H_C])
    wg = wg.at[:, :, 1].set(w_in_c[0][:, 2 * INNER_C + H_C:])
    wg_bf = wg.reshape(D_MODEL, H_C * TILE).astype(BF16)
    bg = jnp.zeros((H_C, TILE), F32).at[:, 0].set(b_gate_c[0][:H_C]).at[:, 1].set(b_gate_c[0][H_C:])
    bg = bg.reshape(1, H_C * TILE)
    cw, cb = conv_w[0], conv_b[0].reshape(1, INNER_C)
    wq_bf, wk_bf, wv_bf = w_q[0].astype(BF16), w_k[0].astype(BF16), w_v[0].astype(BF16)
    og, sk = outnorm_g[0].reshape(1, INNER_C), skip_c[0].reshape(1, INNER_C)
    w_out_c_bf = w_out_c[0].astype(BF16)
    fg = final_norm_g.reshape(1, D_MODEL)

    xm_p, z_p, gate_p = _norm_proj1(h1_p, g1, w_in_c_bf, wg_bf, bg, tm)
    ca_p, q1_p, k1_p, v1_p = _conv_qkv_prompt(xm_p, cw, cb, wq_bf, wk_bf, wv_bf, tm)
    hn_p, C_p, n_p, m_p = _mlstm_prompt(q1_p, k1_p, v1_p, gate_p, lower, sel, nb, lp)
    nc = lp // TILE
    y_prompt = _tail1(hn_p, ca_p, z_p, h1_p, og, sk, w_out_c_bf, fg, (nb, nc - 1),
                      lambda b, i: (b * nc + 1 + i, 0), lambda b, i: (b * (nc - 1) + i, 0), nb * seq, TILE)

    xm_s, z_s, gate_s = _norm_proj1(h1_s, g1, w_in_c_bf, wg_bf, bg, tm)
    conv_ext = jnp.concatenate([state_conv[0].transpose(1, 0, 2), xm_s.reshape(ts, bs, INNER_C)], axis=0)
    ca_s, q1_s, k1_s, v1_s = _conv_qkv_sample(conv_ext, cw, cb, wq_bf, wk_bf, wv_bf)
    rp = 16

    def to_rows(a):
        return jnp.pad(to_sm(a), ((0, 0), (0, rp - ts), (0, 0)))

    m0 = jnp.broadcast_to(state_m[0].reshape(bs * H_C, 1, 1), (bs * H_C, 1, TILE))
    hn_s, C_s, n_s, m_s = _mlstm_sample(to_rows(q1_s), to_rows(k1_s), to_rows(v1_s), to_rows(gate_s),
                                        state_C[0].reshape(bs * H_C, DH_C, DH_C),
                                        state_n[0].reshape(bs * H_C, 1, DH_C), m0, lower, sel, ts)
    y_s = _tail1(to_tm(hn_s[:, :ts]), ca_s, z_s, h1_s, og, sk, w_out_c_bf, fg, (ts * bs // tm,),
                 lambda i: (i, 0), lambda i: (i, 0), ts * bs, tm)

    def heads(a):
        return a.reshape(a.shape[:-1] + (H_A, DH_A))

    k3, v3, u3 = (a.reshape(nb, lp, -1) for a in (k_p, v_p, u_p))
    return (y_prompt.reshape(nb, seq, D_MODEL), to_sm(y_s),
            heads(k3[:, PAD:])[None], heads(v3[:, PAD:])[None],
            heads(to_sm(k_s))[None], heads(to_sm(v_s))[None],
            u3[:, lp - (POOL_MAX - 1):][None], pool_ext[ts:].transpose(1, 0, 2)[None],
            C_p.reshape(1, nb, H_C, DH_C, DH_C), C_s.reshape(1, bs, H_C, DH_C, DH_C),
            n_p.reshape(1, nb, H_C, DH_C), n_s.reshape(1, bs, H_C, DH_C),
            m_p[:, 0, 0].reshape(1, nb, H_C), m_s[:, 0, 0].reshape(1, bs, H_C),
            xm_p.reshape(nb, lp, INNER_C)[:, lp - (CONV_W - 1):][None], conv_ext[ts:].transpose(1, 0, 2)[None])
```

```python
import functools

import jax
import jax.numpy as jnp
from jax import lax
from jax.experimental import pallas as pl
from jax.experimental.pallas import tpu as pltpu

F32 = jnp.float32
BF16 = jnp.bfloat16

D_MODEL = 1024
N_META = 16
W_A = 512
H_A = 8
DH_A = 64
W_B = 512
POOL_WINDOWS = (2, 4, 8, 16)
POOL_GC = 128
POOL_MAX = 16
INNER_C = 2048
H_C = 4
DH_C = 512
CONV_W = 4
EPS = 1e-6

TILE = 128
PAD = TILE - N_META
NEG = -1e30
VMEM_LIMIT = 56 * 1024 * 1024


def _cparams(*sem):
    return pltpu.CompilerParams(dimension_semantics=sem, vmem_limit_bytes=VMEM_LIMIT)


def _silu(x):
    return x / (1.0 + jnp.exp(-x))


def _softplus(z):
    return jnp.maximum(z, 0.0) + jnp.log1p(jnp.exp(-jnp.abs(z)))


def _split2(x):
    hi = x.astype(BF16)
    lo = (x - hi.astype(F32)).astype(BF16)
    return hi, lo


def _split3(x):
    hi = x.astype(BF16)
    r = x - hi.astype(F32)
    mid = r.astype(BF16)
    lo = (r - mid.astype(F32)).astype(BF16)
    return hi, mid, lo


def _dot(a, b):
    return jnp.dot(a, b, preferred_element_type=F32)


def _dot_nt(a, b):
    return lax.dot_general(a, b, (((1,), (1,)), ((), ())), preferred_element_type=F32)


def _dot_tn(a, b):
    return lax.dot_general(a, b, (((0,), (0,)), ((), ())), preferred_element_type=F32)


def _full(shape):
    nd = len(shape)
    return pl.BlockSpec(shape, lambda *_: (0,) * nd)


def _norm_proj_kernel(x_ref, g_ref, w_ref, *out_refs, outs):
    x = x_ref[...]
    ms = jnp.mean(x * x, axis=-1, keepdims=True)
    xn = (x * lax.rsqrt(ms + EPS) * g_ref[...]).astype(BF16)
    for o_ref, (off, width, scale) in zip(out_refs, outs):
        acc = _dot(xn, w_ref[:, off:off + width])
        if scale != 1.0:
            acc = acc * scale
        o_ref[...] = acc.astype(o_ref.dtype)


def _norm_proj(x, g, w_bf, outs, tm):
    rows = x.shape[0]
    kern = functools.partial(_norm_proj_kernel, outs=[(o, w, s) for o, w, s, _ in outs])
    return pl.pallas_call(
        kern,
        grid=(rows // tm,),
        in_specs=[pl.BlockSpec((tm, D_MODEL), lambda i: (i, 0)), _full((1, D_MODEL)), _full(w_bf.shape)],
        out_specs=[pl.BlockSpec((tm, w), lambda i: (i, 0)) for _, w, _, _ in outs],
        out_shape=[jax.ShapeDtypeStruct((rows, w), dt) for _, w, _, dt in outs],
        compiler_params=_cparams("parallel"),
        name="norm_proj",
    )(x, g, w_bf)


def _norm_proj1_kernel(x_ref, g_ref, w_ref, wg_ref, bg_ref, xm_ref, z_ref, gate_ref):
    x = x_ref[...]
    ms = jnp.mean(x * x, axis=-1, keepdims=True)
    xn = (x * lax.rsqrt(ms + EPS) * g_ref[...]).astype(BF16)
    xm_ref[...] = _dot(xn, w_ref[:, :INNER_C])
    z_ref[...] = _dot(xn, w_ref[:, INNER_C:])
    gl = _dot(xn, wg_ref[...]) + bg_ref[...]
    lane = lax.broadcasted_iota(jnp.int32, gl.shape, 1) % TILE
    gate_ref[...] = jnp.where(lane == 1, -_softplus(-gl), gl)


def _norm_proj1(x, g, w_bf, wg_bf, bg, tm):
    rows = x.shape[0]
    return pl.pallas_call(
        _norm_proj1_kernel,
        grid=(rows // tm,),
        in_specs=[pl.BlockSpec((tm, D_MODEL), lambda i: (i, 0)), _full((1, D_MODEL)), _full(w_bf.shape),
                  _full(wg_bf.shape), _full(bg.shape)],
        out_specs=[pl.BlockSpec((tm, INNER_C), lambda i: (i, 0)), pl.BlockSpec((tm, INNER_C), lambda i: (i, 0)),
                   pl.BlockSpec((tm, H_C * TILE), lambda i: (i, 0))],
        out_shape=[jax.ShapeDtypeStruct((rows, INNER_C), F32), jax.ShapeDtypeStruct((rows, INNER_C), F32),
                   jax.ShapeDtypeStruct((rows, H_C * TILE), F32)],
        compiler_params=_cparams("parallel"),
        name="norm_proj1",
    )(x, g, w_bf, wg_bf, bg)


def _sb_block(z, mask, cb, uj):
    sp = _softplus(z)
    lneg = -sp if mask is None else jnp.where(mask, -sp, 0.0)
    hi, lo = _split2(lneg)
    t = _dot(jnp.concatenate([hi, lo], axis=1), uj)
    a = jnp.exp(z - sp + t[:, :TILE] + cb)
    if mask is not None:
        a = jnp.where(mask, a, 0.0)
    return a, cb + t[:, TILE:]


def _sb_prompt_kernel(bias_ref, q_ref, k_ref, v_ref, uj_ref, o_ref, qm_ref):
    qi = pl.program_id(1)
    uj = uj_ref[...]
    lane = lax.broadcasted_iota(jnp.int32, (TILE, TILE), 1)
    first_half = lane < DH_A
    causal = lane < lax.broadcasted_iota(jnp.int32, (TILE, TILE), 0)
    for p in range(H_A // 2):
        qp = q_ref[:, p * TILE:(p + 1) * TILE]
        qm_ref[2 * p] = jnp.where(first_half, qp, jnp.zeros_like(qp))
        qm_ref[2 * p + 1] = jnp.where(first_half, jnp.zeros_like(qp), qp)

    def key_tile(ks, mask, accs, cbs):
        heads = range(H_A)
        kps = [k_ref[pl.ds(ks, TILE), p * TILE:(p + 1) * TILE] for p in range(H_A // 2)]
        vps = [v_ref[pl.ds(ks, TILE), p * TILE:(p + 1) * TILE] for p in range(H_A // 2)]
        zs = [_dot_nt(qm_ref[h], kps[h // 2]) + bias_ref[h] for h in heads]
        sps = [_softplus(z) for z in zs]
        lnegs = [-sp if mask is None else jnp.where(mask, -sp, 0.0) for sp in sps]
        ts = [_dot(jnp.concatenate(_split2(lneg), axis=1), uj) for lneg in lnegs]
        ws = [jnp.exp(zs[h] - sps[h] + ts[h][:, :TILE] + cbs[h]) for h in heads]
        if mask is not None:
            ws = [jnp.where(mask, a, 0.0) for a in ws]
        res = [_dot(ws[h].astype(BF16), vps[h // 2]) for h in heads]
        new_accs = tuple(accs[p] + jnp.where(first_half, res[2 * p], res[2 * p + 1]) for p in range(H_A // 2))
        new_cbs = tuple(cbs[h] + ts[h][:, TILE:] for h in heads)
        return new_accs, new_cbs

    zero = jnp.zeros((TILE, TILE), F32)
    carry = key_tile(pl.multiple_of(qi * TILE, TILE), causal, (zero,) * (H_A // 2), (zero,) * H_A)

    def body(j, carry):
        return key_tile(pl.multiple_of((qi - 1 - j) * TILE, TILE), None, *carry)

    accs, _ = lax.fori_loop(0, qi, body, carry)
    for p in range(H_A // 2):
        o_ref[:, p * TILE:(p + 1) * TILE] = accs[p]


def _sb_prompt(bias, q_bf, k_bf, v_bf, uj, nb, lp):
    nq = lp // TILE
    return pl.pallas_call(
        _sb_prompt_kernel,
        grid=(nb, nq),
        in_specs=[pl.BlockSpec(memory_space=pltpu.SMEM),
                  pl.BlockSpec((TILE, W_A), lambda b, i: (b * nq + i, 0)),
                  pl.BlockSpec((lp, W_A), lambda b, i: (b, 0)),
                  pl.BlockSpec((lp, W_A), lambda b, i: (b, 0)),
                  _full(uj.shape)],
        out_specs=pl.BlockSpec((TILE, W_A), lambda b, i: (b * nq + i, 0)),
        out_shape=jax.ShapeDtypeStruct((nb * lp, W_A), F32),
        scratch_shapes=[pltpu.VMEM((H_A, TILE, TILE), BF16)],
        compiler_params=_cparams("parallel", "arbitrary"),
        name="sb_prompt",
    )(bias, q_bf, k_bf, v_bf, uj)


def _sb_sample_kernel(pt_ref, q_ref, kn_ref, vn_ref, hm_ref, bias_ref, uj_ref, *rest, n_pages, ts):
    kp_refs = rest[:n_pages]
    vp_refs = rest[n_pages:2 * n_pages]
    o_ref = rest[2 * n_pages]
    hm = hm_ref[...]
    qbd = q_ref[...] * hm
    bias = bias_ref[...]
    uj = uj_ref[...]
    rows = ts * H_A
    t_of_row = lax.broadcasted_iota(jnp.int32, (rows, 1), 0) // H_A
    out = jnp.zeros((rows, W_A), F32)
    carry = jnp.zeros((rows, 1), F32)
    for j in range(ts - 1, -1, -1):
        z = jnp.sum(qbd * kn_ref[j:j + 1, :], axis=-1, keepdims=True) + bias[:, 0:1]
        sp = _softplus(z)
        m = j < t_of_row
        a = jnp.where(m, jnp.exp(z - sp + carry), 0.0)
        carry = carry + jnp.where(m, -sp, 0.0)
        out = out + a * vn_ref[j:j + 1, :]
    cb = jnp.broadcast_to(carry, (rows, TILE))
    qbd_bf = qbd.astype(BF16)
    pages = range(n_pages - 1, -1, -1)
    zs = [_dot(qbd_bf, kp_refs[p][...].astype(BF16)) + bias for p in pages]
    sps = [_softplus(z) for z in zs]
    tails = [_dot(jnp.concatenate(_split2(-sp), axis=1), uj) for sp in sps]
    for i, p in enumerate(pages):
        a = jnp.exp(zs[i] - sps[i] + tails[i][:, :TILE] + cb)
        cb = cb + tails[i][:, TILE:]
        out = out + _dot_nt(a.astype(BF16), vp_refs[p][...].astype(BF16))
    o_ref[...] = jnp.sum((out * hm).reshape(ts, H_A, W_A), axis=1)


def _sb_sample(page_table, q_rep, kn, vn, hm, bias_rows, uj, cache_k, cache_v):
    bs, n_pages = page_table.shape
    ts = kn.shape[1]
    page = cache_k.shape[2]
    rows = ts * H_A

    def page_spec(p):
        return pl.BlockSpec((None, W_A, page), lambda s, pt: (pt[s * n_pages + p], 0, 0))

    grid_spec = pltpu.PrefetchScalarGridSpec(
        num_scalar_prefetch=1,
        grid=(bs,),
        in_specs=[pl.BlockSpec((None, rows, W_A), lambda s, pt: (s, 0, 0)),
                  pl.BlockSpec((None, ts, W_A), lambda s, pt: (s, 0, 0)),
                  pl.BlockSpec((None, ts, W_A), lambda s, pt: (s, 0, 0)),
                  pl.BlockSpec(hm.shape, lambda s, pt: (0, 0)),
                  pl.BlockSpec(bias_rows.shape, lambda s, pt: (0, 0)),
                  pl.BlockSpec(uj.shape, lambda s, pt: (0, 0))]
                 + [page_spec(p) for p in range(n_pages)] * 2,
        out_specs=pl.BlockSpec((None, ts, W_A), lambda s, pt: (s, 0, 0)),
    )
    return pl.pallas_call(
        functools.partial(_sb_sample_kernel, n_pages=n_pages, ts=ts),
        grid_spec=grid_spec,
        out_shape=jax.ShapeDtypeStruct((bs, ts, W_A), F32),
        compiler_params=_cparams("arbitrary"),
        name="sb_sample",
    )(page_table.reshape(-1), q_rep, kn, vn, hm, bias_rows, uj,
      *([cache_k] * n_pages), *([cache_v] * n_pages))


def _tail0_math(o_a, g_a, u, win_sums, cnts, g_b, h, wp_ref, ps, wo_ref):
    parts = [(o_a * _silu(g_a)).astype(BF16)]
    for g in range(len(POOL_WINDOWS)):
        cs = slice(g * POOL_GC, (g + 1) * POOL_GC)
        diff = win_sums[g] / cnts[g] - u[:, cs]
        y = _dot(diff.astype(BF16), wp_ref[g]) * ps[:, cs]
        parts.append((y * _silu(g_b[:, cs])).astype(BF16))
    mixed = jnp.concatenate(parts, axis=1)
    return h + _dot(mixed, wo_ref[...])


def _tail0_prompt_kernel(oa_ref, ga_ref, u_ref, halo_ref, gb_ref, h_ref, wp_ref, ps_ref, wo_ref, out_ref,
                         ext_ref, *, tm, lp):
    i = pl.program_id(0)
    ext_ref[0:POOL_MAX, :] = halo_ref[...]
    ext_ref[POOL_MAX:POOL_MAX + tm, :] = u_ref[...]
    row = (i * tm) % lp + lax.broadcasted_iota(jnp.int32, (tm, 1), 0)
    pos = jnp.where(row >= lp, row - lp, row) - PAD
    win_sums, cnts = [], []
    for g, w in enumerate(POOL_WINDOWS):
        cs = slice(g * POOL_GC, (g + 1) * POOL_GC)
        s = ext_ref[POOL_MAX:POOL_MAX + tm, cs]
        for j in range(1, w):
            s = s + ext_ref[POOL_MAX - j:POOL_MAX - j + tm, cs]
        win_sums.append(s)
        cnts.append(jnp.clip(pos + 1, 1, w).astype(F32))
    out_ref[...] = _tail0_math(oa_ref[...], ga_ref[...], u_ref[...], win_sums, cnts, gb_ref[...], h_ref[...],
                               wp_ref, ps_ref[...], wo_ref)


def _tail0_prompt(o_a, g_a, u, g_b, h, wp_bf, ps, wo_bf, tm, lp):
    rows = h.shape[0]
    hb = tm // POOL_MAX
    row = lambda i: (i, 0)
    return pl.pallas_call(
        functools.partial(_tail0_prompt_kernel, tm=tm, lp=lp),
        grid=(rows // tm,),
        in_specs=[pl.BlockSpec((tm, W_A), row), pl.BlockSpec((tm, W_A), row), pl.BlockSpec((tm, W_B), row),
                  pl.BlockSpec((POOL_MAX, W_B), lambda i: (jnp.maximum(i * hb - 1, 0), 0)),
                  pl.BlockSpec((tm, W_B), row), pl.BlockSpec((tm, D_MODEL), row),
                  _full(wp_bf.shape), _full(ps.shape), _full(wo_bf.shape)],
        out_specs=pl.BlockSpec((tm, D_MODEL), row),
        out_shape=jax.ShapeDtypeStruct((rows, D_MODEL), F32),
        scratch_shapes=[pltpu.VMEM((POOL_MAX + tm, W_B), F32)],
        compiler_params=_cparams("parallel"),
        name="tail0_prompt",
    )(o_a, g_a, u, u, g_b, h, wp_bf, ps, wo_bf)


def _tail0_sample_kernel(oa_ref, ga_ref, ext_ref, gb_ref, h_ref, wp_ref, ps_ref, wo_ref, out_ref, *, ts, bs):
    pre = POOL_MAX - 1
    u = jnp.concatenate([ext_ref[pre + t] for t in range(ts)], axis=0)
    win_sums, cnts = [], []
    for g, w in enumerate(POOL_WINDOWS):
        cs = slice(g * POOL_GC, (g + 1) * POOL_GC)
        per_t = []
        for t in range(ts):
            s = ext_ref[pre + t, :, cs]
            for j in range(1, w):
                s = s + ext_ref[pre + t - j, :, cs]
            per_t.append(s)
        win_sums.append(jnp.concatenate(per_t, axis=0))
        cnts.append(float(w))
    out_ref[...] = _tail0_math(oa_ref[...], ga_ref[...], u, win_sums, cnts, gb_ref[...], h_ref[...],
                               wp_ref, ps_ref[...], wo_ref)


def _tail0_sample(o_a, g_a, ext_t, g_b, h, wp_bf, ps, wo_bf):
    rows = h.shape[0]
    ts = ext_t.shape[0] - (POOL_MAX - 1)
    args = (o_a, g_a, ext_t, g_b, h, wp_bf, ps, wo_bf)
    return pl.pallas_call(
        functools.partial(_tail0_sample_kernel, ts=ts, bs=ext_t.shape[1]),
        grid=(1,),
        in_specs=[_full(a.shape) for a in args],
        out_specs=_full((rows, D_MODEL)),
        out_shape=jax.ShapeDtypeStruct((rows, D_MODEL), F32),
        compiler_params=_cparams("arbitrary"),
        name="tail0_sample",
    )(*args)


def _qkv_math(ca, xm, wq_ref, wk_ref, wv_ref, q_ref, k_ref, v_ref):
    ca_bf = ca.astype(BF16)
    xm_bf = xm.astype(BF16)
    for h in range(H_C):
        cs = slice(h * DH_C, (h + 1) * DH_C)
        q_ref[:, cs] = _dot(ca_bf[:, cs], wq_ref[h]).astype(BF16)
        k_ref[:, cs] = (_dot(ca_bf[:, cs], wk_ref[h]) * (DH_C ** -0.5)).astype(BF16)
        v_ref[:, cs] = _dot(xm_bf[:, cs], wv_ref[h]).astype(BF16)


def _conv_qkv_prompt_kernel(xm_ref, halo_ref, cw_ref, cb_ref, wq_ref, wk_ref, wv_ref,
                            ca_ref, q_ref, k_ref, v_ref, ext_ref, *, tm):
    ext_ref[0:8, :] = halo_ref[...]
    ext_ref[8:8 + tm, :] = xm_ref[...]
    conv = jnp.broadcast_to(cb_ref[...], (tm, INNER_C))
    for j in range(CONV_W):
        s = 8 - (CONV_W - 1) + j
        conv = conv + ext_ref[s:s + tm, :] * cw_ref[j:j + 1, :]
    ca = _silu(conv)
    ca_ref[...] = ca
    _qkv_math(ca, xm_ref[...], wq_ref, wk_ref, wv_ref, q_ref, k_ref, v_ref)


def _conv_qkv_prompt(xm, cw, cb, wq_bf, wk_bf, wv_bf, tm):
    rows = xm.shape[0]
    row = lambda i: (i, 0)
    return pl.pallas_call(
        functools.partial(_conv_qkv_prompt_kernel, tm=tm),
        grid=(rows // tm,),
        in_specs=[pl.BlockSpec((tm, INNER_C), row),
                  pl.BlockSpec((8, INNER_C), lambda i: (jnp.maximum(i * (tm // 8) - 1, 0), 0)),
                  _full(cw.shape), _full(cb.shape), _full(wq_bf.shape), _full(wk_bf.shape), _full(wv_bf.shape)],
        out_specs=[pl.BlockSpec((tm, INNER_C), row)] * 4,
        out_shape=[jax.ShapeDtypeStruct((rows, INNER_C), F32)] + [jax.ShapeDtypeStruct((rows, INNER_C), BF16)] * 3,
        scratch_shapes=[pltpu.VMEM((8 + tm, INNER_C), F32)],
        compiler_params=_cparams("parallel"),
        name="conv_qkv_prompt",
    )(xm, xm, cw, cb, wq_bf, wk_bf, wv_bf)


def _conv_qkv_sample_kernel(ext_ref, cw_ref, cb_ref, wq_ref, wk_ref, wv_ref, ca_ref, q_ref, k_ref, v_ref):
    t = pl.program_id(0)
    bs = ext_ref.shape[1]
    conv = jnp.broadcast_to(cb_ref[...], (bs, INNER_C))
    for j in range(CONV_W):
        conv = conv + ext_ref[t + j] * cw_ref[j:j + 1, :]
    ca = _silu(conv)
    ca_ref[...] = ca
    _qkv_math(ca, ext_ref[t + CONV_W - 1], wq_ref, wk_ref, wv_ref, q_ref, k_ref, v_ref)


def _conv_qkv_sample(ext_t, cw, cb, wq_bf, wk_bf, wv_bf):
    ts = ext_t.shape[0] - (CONV_W - 1)
    bs = ext_t.shape[1]
    row = lambda t: (t, 0)
    return pl.pallas_call(
        _conv_qkv_sample_kernel,
        grid=(ts,),
        in_specs=[_full(ext_t.shape), _full(cw.shape), _full(cb.shape), _full(wq_bf.shape), _full(wk_bf.shape),
                  _full(wv_bf.shape)],
        out_specs=[pl.BlockSpec((bs, INNER_C), row)] * 4,
        out_shape=[jax.ShapeDtypeStruct((ts * bs, INNER_C), F32)]
                  + [jax.ShapeDtypeStruct((ts * bs, INNER_C), BF16)] * 3,
        compiler_params=_cparams("arbitrary"),
        name="conv_qkv_sample",
    )(ext_t, cw, cb, wq_bf, wk_bf, wv_bf)


def _mlstm_chunk(q, k, v, g, valid, C, n, m, lower, sel):
    T = q.shape[0]
    lane = lax.broadcasted_iota(jnp.int32, g.shape, 1)
    g = jnp.where(valid, g, jnp.where(lane == 0, NEG, 0.0))
    g_parts = _split3(g)
    cum = sum(_dot(lower, p) for p in g_parts)
    rows_g = sum(_dot_nt(sel, p) for p in g_parts)
    rows_c = sum(_dot_nt(sel, p) for p in _split3(cum))
    logi_c, b_c = g[:, 0:1], cum[:, 1:2]
    logi_r, b_r = rows_g[0:1, :], rows_c[1:2, :]
    tri = lax.broadcasted_iota(jnp.int32, (T, T), 1) <= lax.broadcasted_iota(jnp.int32, (T, T), 0)
    d = jnp.where(tri, b_c - b_r + logi_r, NEG)
    m_inter = b_c + m
    m_t = jnp.maximum(m_inter, jnp.max(d, axis=-1, keepdims=True))
    w_inter = jnp.exp(m_inter - m_t)
    s = jnp.where(tri, _dot_nt(q, k) * jnp.exp(d - m_t), 0.0)
    qf = q.astype(F32)
    num = w_inter * _dot(q, C.astype(BF16)) + _dot(s.astype(BF16), v)
    den = w_inter * jnp.sum(qf * n, axis=-1, keepdims=True) + jnp.sum(s, axis=-1, keepdims=True)
    h = num / jnp.maximum(jnp.abs(den), jnp.exp(-m_t))
    m_new = m_t[T - 1:T, :]
    b_last = b_c[T - 1:T, :]
    w_s = jnp.exp(b_last - b_c + logi_c - m_new)
    decay = jnp.exp(b_last + m - m_new)
    kw = k.astype(F32) * w_s
    C_new = decay * C + _dot_tn(kw.astype(BF16), v)
    n_new = decay * n + jnp.sum(kw, axis=0, keepdims=True)
    mu = jnp.mean(h, axis=-1, keepdims=True)
    hc = h - mu
    var = jnp.mean(hc * hc, axis=-1, keepdims=True)
    return hc * lax.rsqrt(var + EPS), C_new, n_new, m_new


def _mlstm_prompt_kernel(q_ref, k_ref, v_ref, g_ref, lower_ref, sel_ref, hn_ref, c_ref, n_ref, m_ref):
    c = pl.program_id(2)

    @pl.when(c == 0)
    def _():
        c_ref[...] = jnp.zeros_like(c_ref)
        n_ref[...] = jnp.zeros_like(n_ref)
        m_ref[...] = jnp.zeros_like(m_ref)

    valid = c * TILE + lax.broadcasted_iota(jnp.int32, (TILE, 1), 0) >= PAD
    hn, C, n, m = _mlstm_chunk(q_ref[...], k_ref[...], v_ref[...], g_ref[...], valid,
                               c_ref[...], n_ref[...], m_ref[:, 0:1], lower_ref[...], sel_ref[...])
    hn_ref[...] = hn
    c_ref[...] = C
    n_ref[...] = n
    m_ref[...] = jnp.broadcast_to(m, m_ref.shape)


def _mlstm_prompt(q, k, v, g, lower, sel, nb, lp):
    nc = lp // TILE
    blk = lambda b, h, c: (b * nc + c, h)
    st = lambda b, h, c: (b * H_C + h, 0, 0)
    return pl.pallas_call(
        _mlstm_prompt_kernel,
        grid=(nb, H_C, nc),
        in_specs=[pl.BlockSpec((TILE, DH_C), blk)] * 3
                 + [pl.BlockSpec((TILE, TILE), blk), _full(lower.shape), _full(sel.shape)],
        out_specs=[pl.BlockSpec((TILE, DH_C), blk), pl.BlockSpec((None, DH_C, DH_C), st),
                   pl.BlockSpec((None, 1, DH_C), st), pl.BlockSpec((None, 1, TILE), st)],
        out_shape=[jax.ShapeDtypeStruct((nb * lp, INNER_C), F32),
                   jax.ShapeDtypeStruct((nb * H_C, DH_C, DH_C), F32),
                   jax.ShapeDtypeStruct((nb * H_C, 1, DH_C), F32),
                   jax.ShapeDtypeStruct((nb * H_C, 1, TILE), F32)],
        compiler_params=_cparams("parallel", "parallel", "arbitrary"),
        name="mlstm_prompt",
    )(q, k, v, g, lower, sel)


def _mlstm_sample_kernel(q_ref, k_ref, v_ref, g_ref, c_in, n_in, m_in, lower_ref, sel_ref,
                         hn_ref, c_ref, n_ref, m_ref, *, ts):
    valid = lax.broadcasted_iota(jnp.int32, (q_ref.shape[0], 1), 0) < ts
    hn, C, n, m = _mlstm_chunk(q_ref[...], k_ref[...], v_ref[...], g_ref[...], valid,
                               c_in[...], n_in[...], m_in[:, 0:1], lower_ref[...], sel_ref[...])
    hn_ref[...] = hn
    c_ref[...] = C
    n_ref[...] = n
    m_ref[...] = jnp.broadcast_to(m, m_ref.shape)


def _mlstm_sample(q, k, v, g, c0, n0, m0, lower, sel, ts):
    bs, rp = q.shape[0], q.shape[1]
    blk = lambda s, h: (s, 0, h)
    st = lambda s, h: (s * H_C + h, 0, 0)
    return pl.pallas_call(
        functools.partial(_mlstm_sample_kernel, ts=ts),
        grid=(bs, H_C),
        in_specs=[pl.BlockSpec((None, rp, DH_C), blk)] * 3
                 + [pl.BlockSpec((None, rp, TILE), blk), pl.BlockSpec((None, DH_C, DH_C), st),
                    pl.BlockSpec((None, 1, DH_C), st), pl.BlockSpec((None, 1, TILE), st),
                    pl.BlockSpec(lower.shape, lambda s, h: (0, 0)), pl.BlockSpec(sel.shape, lambda s, h: (0, 0))],
        out_specs=[pl.BlockSpec((None, rp, DH_C), blk), pl.BlockSpec((None, DH_C, DH_C), st),
                   pl.BlockSpec((None, 1, DH_C), st), pl.BlockSpec((None, 1, TILE), st)],
        out_shape=[jax.ShapeDtypeStruct((bs, rp, INNER_C), F32),
                   jax.ShapeDtypeStruct((bs * H_C, DH_C, DH_C), F32),
                   jax.ShapeDtypeStruct((bs * H_C, 1, DH_C), F32),
                   jax.ShapeDtypeStruct((bs * H_C, 1, TILE), F32)],
        compiler_params=_cparams("parallel", "arbitrary"),
        name="mlstm_sample",
    )(q, k, v, g, c0, n0, m0, lower, sel)


def _tail1_kernel(hn_ref, ca_ref, z_ref, h_ref, og_ref, sk_ref, wo_ref, fg_ref, y_ref):
    mix = (hn_ref[...] * og_ref[...] + sk_ref[...] * ca_ref[...]) * _silu(z_ref[...])
    h2 = h_ref[...] + _dot(mix.astype(BF16), wo_ref[...])
    ms = jnp.mean(h2 * h2, axis=-1, keepdims=True)
    y_ref[...] = h2 * lax.rsqrt(ms + EPS) * fg_ref[...]


def _tail1(hn, ca, z, h, og, sk, wo_bf, fg, grid, in_map, out_map, out_rows, tm):
    return pl.pallas_call(
        _tail1_kernel,
        grid=grid,
        in_specs=[pl.BlockSpec((tm, INNER_C), in_map)] * 3 + [pl.BlockSpec((tm, D_MODEL), in_map)]
                 + [_full(og.shape), _full(sk.shape), _full(wo_bf.shape), _full(fg.shape)],
        out_specs=pl.BlockSpec((tm, D_MODEL), out_map),
        out_shape=jax.ShapeDtypeStruct((out_rows, D_MODEL), F32),
        compiler_params=_cparams(*(["parallel"] * len(grid))),
        name="tail1",
    )(hn, ca, z, h, og, sk, wo_bf, fg)


def kernel(x_prompt, x_sample, cache_sb_k, cache_sb_v, page_table, state_pool, state_C, state_n, state_m, state_conv, meta_tokens, norm_g, final_norm_g, w_in_a, w_out_a, sb_bias, w_pool, pool_scale, w_in_c, b_gate_c, conv_w, conv_b, w_q, w_k, w_v, skip_c, outnorm_g, w_out_c):
    nb, seq, _ = x_prompt.shape
    bs, ts, _ = x_sample.shape
    lp = PAD + N_META + seq
    n_pool, page = cache_sb_k.shape[1], cache_sb_k.shape[2]
    tm = 2 * TILE

    r = jnp.arange(TILE)
    strict = (r[:, None] > r[None, :]).astype(BF16)
    uj1 = jnp.concatenate([strict, jnp.ones((TILE, TILE), BF16)], axis=1)
    uj = jnp.concatenate([uj1, uj1], axis=0)
    lower = (r[None, :] <= r[:, None]).astype(BF16)
    sel = (jnp.arange(16)[:, None] == r[None, :]).astype(BF16)

    w_in_a_bf = w_in_a[0].astype(BF16)
    w_out_a_bf = w_out_a[0].astype(BF16)
    w_pool_bf = w_pool[0].astype(BF16)
    ps = pool_scale[0].reshape(1, W_B)
    g0 = norm_g[0].reshape(1, D_MODEL)
    outs0 = [(0, W_A, DH_A ** -0.5, BF16),
             (W_A, W_A, 1.0, F32), (2 * W_A, W_A, 1.0, F32),
             (W_A, W_A, 1.0, BF16), (2 * W_A, W_A, 1.0, BF16),
             (3 * W_A, W_A, 1.0, F32),
             (4 * W_A, W_B, 1.0, F32), (4 * W_A + W_B, W_B, 1.0, F32)]

    meta = jnp.broadcast_to(meta_tokens.astype(x_prompt.dtype)[None], (nb, N_META, D_MODEL))
    h_p = jnp.concatenate([jnp.zeros((nb, PAD, D_MODEL), x_prompt.dtype), meta, x_prompt], axis=1)
    h_p = h_p.reshape(nb * lp, D_MODEL)
    q_bf, k_p, v_p, k_bf, v_bf, ga_p, u_p, gb_p = _norm_proj(h_p, g0, w_in_a_bf, outs0, tm)
    oa_p = _sb_prompt(sb_bias[0], q_bf, k_bf, v_bf, uj, nb, lp)
    h1_p = _tail0_prompt(oa_p, ga_p, u_p, gb_p, h_p, w_pool_bf, ps, w_out_a_bf, tm, lp)

    def to_tm(a):
        return a.transpose(1, 0, 2).reshape(ts * bs, a.shape[-1])

    def to_sm(a):
        return a.reshape(ts, bs, a.shape[-1]).transpose(1, 0, 2)

    h_s = to_tm(x_sample)
    outs0_s = [(0, W_A, DH_A ** -0.5, F32)] + outs0[1:3] + outs0[5:]
    q_s, k_s, v_s, ga_s, u_s, gb_s = _norm_proj(h_s, g0, w_in_a_bf, outs0_s, tm)
    q_rep = jnp.repeat(to_sm(q_s), H_A, axis=1)
    row_head = jnp.arange(ts * H_A) % H_A
    hm = (row_head[:, None] == (jnp.arange(W_A) // DH_A)[None, :]).astype(F32)
    bias_rows = jnp.broadcast_to(sb_bias[0][row_head][:, None], (ts * H_A, TILE))
    def pages_t(c):
        return c.transpose(0, 2, 3, 1).reshape(n_pool, W_A, page)

    oa_s = _sb_sample(page_table, q_rep, to_sm(k_s), to_sm(v_s), hm, bias_rows, uj,
                      pages_t(cache_sb_k[0]), pages_t(cache_sb_v[0]))
    pool_ext = jnp.concatenate([state_pool[0].transpose(1, 0, 2), u_s.reshape(ts, bs, W_B)], axis=0)
    h1_s = _tail0_sample(to_tm(oa_s), ga_s, pool_ext, gb_s, h_s, w_pool_bf, ps, w_out_a_bf)

    g1 = norm_g[1].reshape(1, D_MODEL)
    w_in_c_bf = w_in_c[0][:, :2 * INNER_C].astype(BF16)
    wg = jnp.zeros((D_MODEL, H_C, TILE), F32)
    wg = wg.at[:, :, 0].set(w_in_c[0][:, 2 * INNER_C:2 * INNER_C + H_C])
    wg = wg.at[:, :, 1].set(w_in_c[0][:, 2 * INNER_C + H_C:])
    wg_bf = wg.reshape(D_MODEL, H_C * TILE).astype(BF16)
    bg = jnp.zeros((H_C, TILE), F32).at[:, 0].set(b_gate_c[0][:H_C]).at[:, 1].set(b_gate_c[0][H_C:])
    bg = bg.reshape(1, H_C * TILE)
    cw, cb = conv_w[0], conv_b[0].reshape(1, INNER_C)
    wq_bf, wk_bf, wv_bf = w_q[0].astype(BF16), w_k[0].astype(BF16), w_v[0].astype(BF16)
    og, sk = outnorm_g[0].reshape(1, INNER_C), skip_c[0].reshape(1, INNER_C)
    w_out_c_bf = w_out_c[0].astype(BF16)
    fg = final_norm_g.reshape(1, D_MODEL)

    xm_p, z_p, gate_p = _norm_proj1(h1_p, g1, w_in_c_bf, wg_bf, bg, tm)
    ca_p, q1_p, k1_p, v1_p = _conv_qkv_prompt(xm_p, cw, cb, wq_bf, wk_bf, wv_bf, tm)
    hn_p, C_p, n_p, m_p = _mlstm_prompt(q1_p, k1_p, v1_p, gate_p, lower, sel, nb, lp)
    nc = lp // TILE
    y_prompt = _tail1(hn_p, ca_p, z_p, h1_p, og, sk, w_out_c_bf, fg, (nb, nc - 1),
                      lambda b, i: (b * nc + 1 + i, 0), lambda b, i: (b * (nc - 1) + i, 0), nb * seq, TILE)

    xm_s, z_s, gate_s = _norm_proj1(h1_s, g1, w_in_c_bf, wg_bf, bg, tm)
    conv_ext = jnp.concatenate([state_conv[0].transpose(1, 0, 2), xm_s.reshape(ts, bs, INNER_C)], axis=0)
    ca_s, q1_s, k1_s, v1_s = _conv_qkv_sample(conv_ext, cw, cb, wq_bf, wk_bf, wv_bf)
    rp = 16

    def to_rows(a):
        return jnp.pad(to_sm(a), ((0, 0), (0, rp - ts), (0, 0)))

    m0 = jnp.broadcast_to(state_m[0].reshape(bs * H_C, 1, 1), (bs * H_C, 1, TILE))
    hn_s, C_s, n_s, m_s = _mlstm_sample(to_rows(q1_s), to_rows(k1_s), to_rows(v1_s), to_rows(gate_s),
                                        state_C[0].reshape(bs * H_C, DH_C, DH_C),
                                        state_n[0].reshape(bs * H_C, 1, DH_C), m0, lower[:rp, :rp], sel, ts)
    y_s = _tail1(to_tm(hn_s[:, :ts]), ca_s, z_s, h1_s, og, sk, w_out_c_bf, fg, (ts * bs // tm,),
                 lambda i: (i, 0), lambda i: (i, 0), ts * bs, tm)

    def heads(a):
        return a.reshape(a.shape[:-1] + (H_A, DH_A))

    k3, v3, u3 = (a.reshape(nb, lp, -1) for a in (k_p, v_p, u_p))
    return (y_prompt.reshape(nb, seq, D_MODEL), to_sm(y_s),
            heads(k3[:, PAD:])[None], heads(v3[:, PAD:])[None],
            heads(to_sm(k_s))[None], heads(to_sm(v_s))[None],
            u3[:, lp - (POOL_MAX - 1):][None], pool_ext[ts:].transpose(1, 0, 2)[None],
            C_p.reshape(1, nb, H_C, DH_C, DH_C), C_s.reshape(1, bs, H_C, DH_C, DH_C),
            n_p.reshape(1, nb, H_C, DH_C), n_s.reshape(1, bs, H_C, DH_C),
            m_p[:, 0, 0].reshape(1, nb, H_C), m_s[:, 0, 0].reshape(1, bs, H_C),
            xm_p.reshape(nb, lp, INNER_C)[:, lp - (CONV_W - 1):][None], conv_ext[ts:].transpose(1, 0, 2)[None])
```

```python
import functools

import jax
import jax.numpy as jnp
from jax import lax
from jax.experimental import pallas as pl
from jax.experimental.pallas import tpu as pltpu

F32 = jnp.float32
BF16 = jnp.bfloat16

D_MODEL = 1024
N_META = 16
W_A = 512
H_A = 8
DH_A = 64
W_B = 512
POOL_WINDOWS = (2, 4, 8, 16)
POOL_GC = 128
POOL_MAX = 16
INNER_C = 2048
H_C = 4
DH_C = 512
CONV_W = 4
EPS = 1e-6

TILE = 128
PAD = TILE - N_META
NEG = -1e30
VMEM_LIMIT = 56 * 1024 * 1024


def _cparams(*sem):
    return pltpu.CompilerParams(dimension_semantics=sem, vmem_limit_bytes=VMEM_LIMIT)


def _silu(x):
    return x / (1.0 + jnp.exp(-x))


def _softplus(z):
    return jnp.maximum(z, 0.0) + jnp.log(1.0 + jnp.exp(-jnp.abs(z)))


def _split2(x):
    hi = x.astype(BF16)
    lo = (x - hi.astype(F32)).astype(BF16)
    return hi, lo


def _split3(x):
    hi = x.astype(BF16)
    r = x - hi.astype(F32)
    mid = r.astype(BF16)
    lo = (r - mid.astype(F32)).astype(BF16)
    return hi, mid, lo


def _dot(a, b):
    return jnp.dot(a, b, preferred_element_type=F32)


def _dot_nt(a, b):
    return lax.dot_general(a, b, (((1,), (1,)), ((), ())), preferred_element_type=F32)


def _dot_tn(a, b):
    return lax.dot_general(a, b, (((0,), (0,)), ((), ())), preferred_element_type=F32)


def _full(shape):
    nd = len(shape)
    return pl.BlockSpec(shape, lambda *_: (0,) * nd)


def _norm_proj_kernel(x_ref, g_ref, w_ref, *out_refs, outs):
    x = x_ref[...]
    ms = jnp.mean(x * x, axis=-1, keepdims=True)
    xn = (x * lax.rsqrt(ms + EPS) * g_ref[...]).astype(BF16)
    for o_ref, (off, width, scale) in zip(out_refs, outs):
        acc = _dot(xn, w_ref[:, off:off + width])
        if scale != 1.0:
            acc = acc * scale
        o_ref[...] = acc.astype(o_ref.dtype)


def _norm_proj(x, g, w_bf, outs, tm):
    rows = x.shape[0]
    kern = functools.partial(_norm_proj_kernel, outs=[(o, w, s) for o, w, s, _ in outs])
    return pl.pallas_call(
        kern,
        grid=(rows // tm,),
        in_specs=[pl.BlockSpec((tm, D_MODEL), lambda i: (i, 0)), _full((1, D_MODEL)), _full(w_bf.shape)],
        out_specs=[pl.BlockSpec((tm, w), lambda i: (i, 0)) for _, w, _, _ in outs],
        out_shape=[jax.ShapeDtypeStruct((rows, w), dt) for _, w, _, dt in outs],
        compiler_params=_cparams("parallel"),
        name="norm_proj",
    )(x, g, w_bf)


def _norm_proj1_kernel(x_ref, g_ref, w_ref, wg_ref, bg_ref, xm_ref, z_ref, gate_ref):
    x = x_ref[...]
    ms = jnp.mean(x * x, axis=-1, keepdims=True)
    xn = (x * lax.rsqrt(ms + EPS) * g_ref[...]).astype(BF16)
    xm_ref[...] = _dot(xn, w_ref[:, :INNER_C])
    z_ref[...] = _dot(xn, w_ref[:, INNER_C:])
    gl = _dot(xn, wg_ref[...]) + bg_ref[...]
    lane = lax.broadcasted_iota(jnp.int32, gl.shape, 1) % TILE
    gate_ref[...] = jnp.where(lane == 1, -_softplus(-gl), gl)


def _norm_proj1(x, g, w_bf, wg_bf, bg, tm):
    rows = x.shape[0]
    return pl.pallas_call(
        _norm_proj1_kernel,
        grid=(rows // tm,),
        in_specs=[pl.BlockSpec((tm, D_MODEL), lambda i: (i, 0)), _full((1, D_MODEL)), _full(w_bf.shape),
                  _full(wg_bf.shape), _full(bg.shape)],
        out_specs=[pl.BlockSpec((tm, INNER_C), lambda i: (i, 0)), pl.BlockSpec((tm, INNER_C), lambda i: (i, 0)),
                   pl.BlockSpec((tm, H_C * TILE), lambda i: (i, 0))],
        out_shape=[jax.ShapeDtypeStruct((rows, INNER_C), F32), jax.ShapeDtypeStruct((rows, INNER_C), F32),
                   jax.ShapeDtypeStruct((rows, H_C * TILE), F32)],
        compiler_params=_cparams("parallel"),
        name="norm_proj1",
    )(x, g, w_bf, wg_bf, bg)


def _sb_block(z, mask, cb, uj):
    sp = _softplus(z)
    lneg = -sp if mask is None else jnp.where(mask, -sp, 0.0)
    hi, lo = _split2(lneg)
    t = _dot(jnp.concatenate([hi, lo], axis=1), uj)
    a = jnp.exp(z - sp + t[:, :TILE] + cb)
    if mask is not None:
        a = jnp.where(mask, a, 0.0)
    return a, cb + t[:, TILE:]


def _sb_prompt_kernel(bias_ref, q_ref, k_ref, v_ref, uj_ref, o_ref, qm_ref):
    qi = pl.program_id(1)
    uj = uj_ref[...]
    lane = lax.broadcasted_iota(jnp.int32, (TILE, TILE), 1)
    first_half = lane < DH_A
    causal = lane < lax.broadcasted_iota(jnp.int32, (TILE, TILE), 0)
    for p in range(H_A // 2):
        qp = q_ref[:, p * TILE:(p + 1) * TILE]
        qm_ref[2 * p] = jnp.where(first_half, qp, jnp.zeros_like(qp))
        qm_ref[2 * p + 1] = jnp.where(first_half, jnp.zeros_like(qp), qp)

    def key_tile(ks, mask, accs, cbs):
        heads = range(H_A)
        kps = [k_ref[pl.ds(ks, TILE), p * TILE:(p + 1) * TILE] for p in range(H_A // 2)]
        vps = [v_ref[pl.ds(ks, TILE), p * TILE:(p + 1) * TILE] for p in range(H_A // 2)]
        zs = [_dot_nt(qm_ref[h], kps[h // 2]) + bias_ref[h] for h in heads]
        sps = [_softplus(z) for z in zs]
        lnegs = [-sp if mask is None else jnp.where(mask, -sp, 0.0) for sp in sps]
        ts = [_dot(jnp.concatenate(_split2(lneg), axis=1), uj) for lneg in lnegs]
        ws = [jnp.exp(zs[h] - sps[h] + ts[h][:, :TILE] + cbs[h]) for h in heads]
        if mask is not None:
            ws = [jnp.where(mask, a, 0.0) for a in ws]
        res = [_dot(ws[h].astype(BF16), vps[h // 2]) for h in heads]
        new_accs = tuple(accs[p] + jnp.where(first_half, res[2 * p], res[2 * p + 1]) for p in range(H_A // 2))
        new_cbs = tuple(cbs[h] + ts[h][:, TILE:] for h in heads)
        return new_accs, new_cbs

    zero = jnp.zeros((TILE, TILE), F32)
    carry = key_tile(pl.multiple_of(qi * TILE, TILE), causal, (zero,) * (H_A // 2), (zero,) * H_A)

    def body(j, carry):
        return key_tile(pl.multiple_of((qi - 1 - j) * TILE, TILE), None, *carry)

    accs, _ = lax.fori_loop(0, qi, body, carry)
    for p in range(H_A // 2):
        o_ref[:, p * TILE:(p + 1) * TILE] = accs[p]


def _sb_prompt(bias, q_bf, k_bf, v_bf, uj, nb, lp):
    nq = lp // TILE
    return pl.pallas_call(
        _sb_prompt_kernel,
        grid=(nb, nq),
        in_specs=[pl.BlockSpec(memory_space=pltpu.SMEM),
                  pl.BlockSpec((TILE, W_A), lambda b, i: (b * nq + i, 0)),
                  pl.BlockSpec((lp, W_A), lambda b, i: (b, 0)),
                  pl.BlockSpec((lp, W_A), lambda b, i: (b, 0)),
                  _full(uj.shape)],
        out_specs=pl.BlockSpec((TILE, W_A), lambda b, i: (b * nq + i, 0)),
        out_shape=jax.ShapeDtypeStruct((nb * lp, W_A), F32),
        scratch_shapes=[pltpu.VMEM((H_A, TILE, TILE), BF16)],
        compiler_params=_cparams("parallel", "arbitrary"),
        name="sb_prompt",
    )(bias, q_bf, k_bf, v_bf, uj)


def _sb_sample_kernel(pt_ref, q_ref, kn_ref, vn_ref, hm_ref, bias_ref, uj_ref, *rest, n_pages, ts):
    kp_refs = rest[:n_pages]
    vp_refs = rest[n_pages:2 * n_pages]
    o_ref = rest[2 * n_pages]
    hm = hm_ref[...]
    qbd = q_ref[...] * hm
    bias = bias_ref[...]
    uj = uj_ref[...]
    rows = ts * H_A
    t_of_row = lax.broadcasted_iota(jnp.int32, (rows, 1), 0) // H_A
    out = jnp.zeros((rows, W_A), F32)
    carry = jnp.zeros((rows, 1), F32)
    for j in range(ts - 1, -1, -1):
        z = jnp.sum(qbd * kn_ref[j:j + 1, :], axis=-1, keepdims=True) + bias[:, 0:1]
        sp = _softplus(z)
        m = j < t_of_row
        a = jnp.where(m, jnp.exp(z - sp + carry), 0.0)
        carry = carry + jnp.where(m, -sp, 0.0)
        out = out + a * vn_ref[j:j + 1, :]
    cb = jnp.broadcast_to(carry, (rows, TILE))
    qbd_bf = qbd.astype(BF16)
    pages = range(n_pages - 1, -1, -1)
    zs = [_dot(qbd_bf, kp_refs[p][...].astype(BF16)) + bias for p in pages]
    sps = [_softplus(z) for z in zs]
    tails = [_dot(jnp.concatenate(_split2(-sp), axis=1), uj) for sp in sps]
    for i, p in enumerate(pages):
        a = jnp.exp(zs[i] - sps[i] + tails[i][:, :TILE] + cb)
        cb = cb + tails[i][:, TILE:]
        out = out + _dot_nt(a.astype(BF16), vp_refs[p][...].astype(BF16))
    o_ref[...] = jnp.sum((out * hm).reshape(ts, H_A, W_A), axis=1)


def _sb_sample(page_table, q_rep, kn, vn, hm, bias_rows, uj, cache_k, cache_v):
    bs, n_pages = page_table.shape
    ts = kn.shape[1]
    page = cache_k.shape[2]
    rows = ts * H_A

    def page_spec(p):
        return pl.BlockSpec((None, W_A, page), lambda s, pt: (pt[s * n_pages + p], 0, 0))

    grid_spec = pltpu.PrefetchScalarGridSpec(
        num_scalar_prefetch=1,
        grid=(bs,),
        in_specs=[pl.BlockSpec((None, rows, W_A), lambda s, pt: (s, 0, 0)),
                  pl.BlockSpec((None, ts, W_A), lambda s, pt: (s, 0, 0)),
                  pl.BlockSpec((None, ts, W_A), lambda s, pt: (s, 0, 0)),
                  pl.BlockSpec(hm.shape, lambda s, pt: (0, 0)),
                  pl.BlockSpec(bias_rows.shape, lambda s, pt: (0, 0)),
                  pl.BlockSpec(uj.shape, lambda s, pt: (0, 0))]
                 + [page_spec(p) for p in range(n_pages)] * 2,
        out_specs=pl.BlockSpec((None, ts, W_A), lambda s, pt: (s, 0, 0)),
    )
    return pl.pallas_call(
        functools.partial(_sb_sample_kernel, n_pages=n_pages, ts=ts),
        grid_spec=grid_spec,
        out_shape=jax.ShapeDtypeStruct((bs, ts, W_A), F32),
        compiler_params=_cparams("arbitrary"),
        name="sb_sample",
    )(page_table.reshape(-1), q_rep, kn, vn, hm, bias_rows, uj,
      *([cache_k] * n_pages), *([cache_v] * n_pages))


def _tail0_math(o_a, g_a, u, win_sums, cnts, g_b, h, wp_ref, ps, wo_ref):
    parts = [(o_a * _silu(g_a)).astype(BF16)]
    for g in range(len(POOL_WINDOWS)):
        cs = slice(g * POOL_GC, (g + 1) * POOL_GC)
        diff = win_sums[g] / cnts[g] - u[:, cs]
        y = _dot(diff.astype(BF16), wp_ref[g]) * ps[:, cs]
        parts.append((y * _silu(g_b[:, cs])).astype(BF16))
    mixed = jnp.concatenate(parts, axis=1)
    return h + _dot(mixed, wo_ref[...])


def _tail0_prompt_kernel(oa_ref, ga_ref, u_ref, halo_ref, gb_ref, h_ref, wp_ref, ps_ref, wo_ref, out_ref,
                         ext_ref, *, tm, lp):
    i = pl.program_id(0)
    ext_ref[0:POOL_MAX, :] = halo_ref[...]
    ext_ref[POOL_MAX:POOL_MAX + tm, :] = u_ref[...]
    row = (i * tm) % lp + lax.broadcasted_iota(jnp.int32, (tm, 1), 0)
    pos = jnp.where(row >= lp, row - lp, row) - PAD
    win_sums, cnts = [], []
    for g, w in enumerate(POOL_WINDOWS):
        cs = slice(g * POOL_GC, (g + 1) * POOL_GC)
        s = ext_ref[POOL_MAX:POOL_MAX + tm, cs]
        for j in range(1, w):
            s = s + ext_ref[POOL_MAX - j:POOL_MAX - j + tm, cs]
        win_sums.append(s)
        cnts.append(jnp.clip(pos + 1, 1, w).astype(F32))
    out_ref[...] = _tail0_math(oa_ref[...], ga_ref[...], u_ref[...], win_sums, cnts, gb_ref[...], h_ref[...],
                               wp_ref, ps_ref[...], wo_ref)


def _tail0_prompt(o_a, g_a, u, g_b, h, wp_bf, ps, wo_bf, tm, lp):
    rows = h.shape[0]
    hb = tm // POOL_MAX
    row = lambda i: (i, 0)
    return pl.pallas_call(
        functools.partial(_tail0_prompt_kernel, tm=tm, lp=lp),
        grid=(rows // tm,),
        in_specs=[pl.BlockSpec((tm, W_A), row), pl.BlockSpec((tm, W_A), row), pl.BlockSpec((tm, W_B), row),
                  pl.BlockSpec((POOL_MAX, W_B), lambda i: (jnp.maximum(i * hb - 1, 0), 0)),
                  pl.BlockSpec((tm, W_B), row), pl.BlockSpec((tm, D_MODEL), row),
                  _full(wp_bf.shape), _full(ps.shape), _full(wo_bf.shape)],
        out_specs=pl.BlockSpec((tm, D_MODEL), row),
        out_shape=jax.ShapeDtypeStruct((rows, D_MODEL), F32),
        scratch_shapes=[pltpu.VMEM((POOL_MAX + tm, W_B), F32)],
        compiler_params=_cparams("parallel"),
        name="tail0_prompt",
    )(o_a, g_a, u, u, g_b, h, wp_bf, ps, wo_bf)


def _tail0_sample_kernel(oa_ref, ga_ref, ext_ref, gb_ref, h_ref, wp_ref, ps_ref, wo_ref, out_ref, *, ts, bs):
    pre = POOL_MAX - 1
    u = jnp.concatenate([ext_ref[pre + t] for t in range(ts)], axis=0)
    win_sums, cnts = [], []
    for g, w in enumerate(POOL_WINDOWS):
        cs = slice(g * POOL_GC, (g + 1) * POOL_GC)
        per_t = []
        for t in range(ts):
            s = ext_ref[pre + t, :, cs]
            for j in range(1, w):
                s = s + ext_ref[pre + t - j, :, cs]
            per_t.append(s)
        win_sums.append(jnp.concatenate(per_t, axis=0))
        cnts.append(float(w))
    out_ref[...] = _tail0_math(oa_ref[...], ga_ref[...], u, win_sums, cnts, gb_ref[...], h_ref[...],
                               wp_ref, ps_ref[...], wo_ref)


def _tail0_sample(o_a, g_a, ext_t, g_b, h, wp_bf, ps, wo_bf):
    rows = h.shape[0]
    ts = ext_t.shape[0] - (POOL_MAX - 1)
    args = (o_a, g_a, ext_t, g_b, h, wp_bf, ps, wo_bf)
    return pl.pallas_call(
        functools.partial(_tail0_sample_kernel, ts=ts, bs=ext_t.shape[1]),
        grid=(1,),
        in_specs=[_full(a.shape) for a in args],
        out_specs=_full((rows, D_MODEL)),
        out_shape=jax.ShapeDtypeStruct((rows, D_MODEL), F32),
        compiler_params=_cparams("arbitrary"),
        name="tail0_sample",
    )(*args)


def _qkv_math(ca, xm, wq_ref, wk_ref, wv_ref, q_ref, k_ref, v_ref):
    ca_bf = ca.astype(BF16)
    xm_bf = xm.astype(BF16)
    for h in range(H_C):
        cs = slice(h * DH_C, (h + 1) * DH_C)
        q_ref[:, cs] = _dot(ca_bf[:, cs], wq_ref[h]).astype(BF16)
        k_ref[:, cs] = (_dot(ca_bf[:, cs], wk_ref[h]) * (DH_C ** -0.5)).astype(BF16)
        v_ref[:, cs] = _dot(xm_bf[:, cs], wv_ref[h]).astype(BF16)


def _conv_qkv_prompt_kernel(xm_ref, halo_ref, cw_ref, cb_ref, wq_ref, wk_ref, wv_ref,
                            ca_ref, q_ref, k_ref, v_ref, ext_ref, *, tm):
    ext_ref[0:8, :] = halo_ref[...]
    ext_ref[8:8 + tm, :] = xm_ref[...]
    conv = jnp.broadcast_to(cb_ref[...], (tm, INNER_C))
    for j in range(CONV_W):
        s = 8 - (CONV_W - 1) + j
        conv = conv + ext_ref[s:s + tm, :] * cw_ref[j:j + 1, :]
    ca = _silu(conv)
    ca_ref[...] = ca
    _qkv_math(ca, xm_ref[...], wq_ref, wk_ref, wv_ref, q_ref, k_ref, v_ref)


def _conv_qkv_prompt(xm, cw, cb, wq_bf, wk_bf, wv_bf, tm):
    rows = xm.shape[0]
    row = lambda i: (i, 0)
    return pl.pallas_call(
        functools.partial(_conv_qkv_prompt_kernel, tm=tm),
        grid=(rows // tm,),
        in_specs=[pl.BlockSpec((tm, INNER_C), row),
                  pl.BlockSpec((8, INNER_C), lambda i: (jnp.maximum(i * (tm // 8) - 1, 0), 0)),
                  _full(cw.shape), _full(cb.shape), _full(wq_bf.shape), _full(wk_bf.shape), _full(wv_bf.shape)],
        out_specs=[pl.BlockSpec((tm, INNER_C), row)] * 4,
        out_shape=[jax.ShapeDtypeStruct((rows, INNER_C), F32)] + [jax.ShapeDtypeStruct((rows, INNER_C), BF16)] * 3,
        scratch_shapes=[pltpu.VMEM((8 + tm, INNER_C), F32)],
        compiler_params=_cparams("parallel"),
        name="conv_qkv_prompt",
    )(xm, xm, cw, cb, wq_bf, wk_bf, wv_bf)


def _conv_qkv_sample_kernel(ext_ref, cw_ref, cb_ref, wq_ref, wk_ref, wv_ref, ca_ref, q_ref, k_ref, v_ref):
    t = pl.program_id(0)
    bs = ext_ref.shape[1]
    conv = jnp.broadcast_to(cb_ref[...], (bs, INNER_C))
    for j in range(CONV_W):
        conv = conv + ext_ref[t + j] * cw_ref[j:j + 1, :]
    ca = _silu(conv)
    ca_ref[...] = ca
    _qkv_math(ca, ext_ref[t + CONV_W - 1], wq_ref, wk_ref, wv_ref, q_ref, k_ref, v_ref)


def _conv_qkv_sample(ext_t, cw, cb, wq_bf, wk_bf, wv_bf):
    ts = ext_t.shape[0] - (CONV_W - 1)
    bs = ext_t.shape[1]
    row = lambda t: (t, 0)
    return pl.pallas_call(
        _conv_qkv_sample_kernel,
        grid=(ts,),
        in_specs=[_full(ext_t.shape), _full(cw.shape), _full(cb.shape), _full(wq_bf.shape), _full(wk_bf.shape),
                  _full(wv_bf.shape)],
        out_specs=[pl.BlockSpec((bs, INNER_C), row)] * 4,
        out_shape=[jax.ShapeDtypeStruct((ts * bs, INNER_C), F32)]
                  + [jax.ShapeDtypeStruct((ts * bs, INNER_C), BF16)] * 3,
        compiler_params=_cparams("arbitrary"),
        name="conv_qkv_sample",
    )(ext_t, cw, cb, wq_bf, wk_bf, wv_bf)


def _mlstm_heads(qs, ks, vs, gs, valid, Cs, ns, ms, lower, sel):
    H = range(len(qs))
    T = qs[0].shape[0]
    lane = lax.broadcasted_iota(jnp.int32, gs[0].shape, 1)
    pad_row = jnp.where(lane == 0, NEG, 0.0)
    gs = [jnp.where(valid, g, pad_row) for g in gs]
    qks = [_dot_nt(qs[h], ks[h]) for h in H]
    qcs = [_dot(qs[h], Cs[h].astype(BF16)) for h in H]
    g_parts = [_split3(g) for g in gs]
    cums = [sum(_dot(lower, p) for p in g_parts[h]) for h in H]
    rows_g = [sum(_dot_nt(sel, p) for p in g_parts[h]) for h in H]
    rows_c = [sum(_dot_nt(sel, p) for p in _split3(cums[h])) for h in H]
    tri = lax.broadcasted_iota(jnp.int32, (T, T), 1) <= lax.broadcasted_iota(jnp.int32, (T, T), 0)
    b_cs = [cums[h][:, 1:2] for h in H]
    ds = [jnp.where(tri, b_cs[h] - rows_c[h][1:2, :] + rows_g[h][0:1, :], NEG) for h in H]
    m_inters = [b_cs[h] + ms[h] for h in H]
    m_ts = [jnp.maximum(m_inters[h], jnp.max(ds[h], axis=-1, keepdims=True)) for h in H]
    ss = [jnp.where(tri, qks[h] * jnp.exp(ds[h] - m_ts[h]), 0.0) for h in H]
    svs = [_dot(ss[h].astype(BF16), vs[h]) for h in H]
    m_news = [m_ts[h][T - 1:T, :] for h in H]
    b_lasts = [b_cs[h][T - 1:T, :] for h in H]
    kws = [ks[h].astype(F32) * jnp.exp(b_lasts[h] - b_cs[h] + gs[h][:, 0:1] - m_news[h]) for h in H]
    upds = [_dot_tn(kws[h].astype(BF16), vs[h]) for h in H]
    decays = [jnp.exp(b_lasts[h] + ms[h] - m_news[h]) for h in H]
    C_news = [decays[h] * Cs[h] + upds[h] for h in H]
    n_news = [decays[h] * ns[h] + jnp.sum(kws[h], axis=0, keepdims=True) for h in H]
    hns = []
    for h in H:
        w_inter = jnp.exp(m_inters[h] - m_ts[h])
        num = w_inter * qcs[h] + svs[h]
        den = (w_inter * jnp.sum(qs[h].astype(F32) * ns[h], axis=-1, keepdims=True)
               + jnp.sum(ss[h], axis=-1, keepdims=True))
        hh = num / jnp.maximum(jnp.abs(den), jnp.exp(-m_ts[h]))
        hc = hh - jnp.mean(hh, axis=-1, keepdims=True)
        var = jnp.mean(hc * hc, axis=-1, keepdims=True)
        hns.append(hc * lax.rsqrt(var + EPS))
    return hns, C_news, n_news, m_news


def _mlstm_step(q_ref, k_ref, v_ref, g_ref, valid, c_in, n_in, m_in, lower_ref, sel_ref, hn_ref, c_ref, n_ref, m_ref):
    H = range(H_C)
    cols = [slice(h * DH_C, (h + 1) * DH_C) for h in H]
    hns, Cs, ns, ms = _mlstm_heads(
        [q_ref[:, c] for c in cols], [k_ref[:, c] for c in cols], [v_ref[:, c] for c in cols],
        [g_ref[:, h * TILE:(h + 1) * TILE] for h in H], valid,
        [c_in[h] for h in H], [n_in[h] for h in H], [m_in[h][:, 0:1] for h in H], lower_ref[...], sel_ref[...])
    for h in H:
        hn_ref[:, cols[h]] = hns[h]
        c_ref[h] = Cs[h]
        n_ref[h] = ns[h]
        m_ref[h] = jnp.broadcast_to(ms[h], (1, TILE))


def _mlstm_prompt_kernel(q_ref, k_ref, v_ref, g_ref, lower_ref, sel_ref, hn_ref, c_ref, n_ref, m_ref):
    c = pl.program_id(1)

    @pl.when(c == 0)
    def _():
        c_ref[...] = jnp.zeros_like(c_ref)
        n_ref[...] = jnp.zeros_like(n_ref)
        m_ref[...] = jnp.zeros_like(m_ref)

    valid = c * TILE + lax.broadcasted_iota(jnp.int32, (TILE, 1), 0) >= PAD
    _mlstm_step(q_ref, k_ref, v_ref, g_ref, valid, c_ref, n_ref, m_ref, lower_ref, sel_ref,
                hn_ref, c_ref, n_ref, m_ref)


def _mlstm_prompt(q, k, v, g, lower, sel, nb, lp):
    nc = lp // TILE
    blk = lambda b, c: (b * nc + c, 0)
    st = lambda b, c: (b, 0, 0)
    return pl.pallas_call(
        _mlstm_prompt_kernel,
        grid=(nb, nc),
        in_specs=[pl.BlockSpec((TILE, INNER_C), blk)] * 3
                 + [pl.BlockSpec((TILE, H_C * TILE), blk), _full(lower.shape), _full(sel.shape)],
        out_specs=[pl.BlockSpec((TILE, INNER_C), blk), pl.BlockSpec((H_C, DH_C, DH_C), st),
                   pl.BlockSpec((H_C, 1, DH_C), st), pl.BlockSpec((H_C, 1, TILE), st)],
        out_shape=[jax.ShapeDtypeStruct((nb * lp, INNER_C), F32),
                   jax.ShapeDtypeStruct((nb * H_C, DH_C, DH_C), F32),
                   jax.ShapeDtypeStruct((nb * H_C, 1, DH_C), F32),
                   jax.ShapeDtypeStruct((nb * H_C, 1, TILE), F32)],
        compiler_params=_cparams("parallel", "arbitrary"),
        name="mlstm_prompt",
    )(q, k, v, g, lower, sel)


def _mlstm_sample_kernel(q_ref, k_ref, v_ref, g_ref, c_in, n_in, m_in, lower_ref, sel_ref,
                         hn_ref, c_ref, n_ref, m_ref, *, ts):
    valid = lax.broadcasted_iota(jnp.int32, (q_ref.shape[0], 1), 0) < ts
    _mlstm_step(q_ref, k_ref, v_ref, g_ref, valid, c_in, n_in, m_in, lower_ref, sel_ref,
                hn_ref, c_ref, n_ref, m_ref)


def _mlstm_sample(q, k, v, g, c0, n0, m0, lower, sel, ts):
    bs, rp = q.shape[0], q.shape[1]
    blk = lambda s: (s, 0, 0)
    state_specs = [pl.BlockSpec((H_C, DH_C, DH_C), blk), pl.BlockSpec((H_C, 1, DH_C), blk),
                   pl.BlockSpec((H_C, 1, TILE), blk)]
    return pl.pallas_call(
        functools.partial(_mlstm_sample_kernel, ts=ts),
        grid=(bs,),
        in_specs=[pl.BlockSpec((None, rp, INNER_C), blk)] * 3 + [pl.BlockSpec((None, rp, H_C * TILE), blk)]
                 + state_specs + [_full(lower.shape), _full(sel.shape)],
        out_specs=[pl.BlockSpec((None, rp, INNER_C), blk)] + state_specs,
        out_shape=[jax.ShapeDtypeStruct((bs, rp, INNER_C), F32),
                   jax.ShapeDtypeStruct((bs * H_C, DH_C, DH_C), F32),
                   jax.ShapeDtypeStruct((bs * H_C, 1, DH_C), F32),
                   jax.ShapeDtypeStruct((bs * H_C, 1, TILE), F32)],
        compiler_params=_cparams("parallel"),
        name="mlstm_sample",
    )(q, k, v, g, c0, n0, m0, lower, sel)


def _tail1_kernel(hn_ref, ca_ref, z_ref, h_ref, og_ref, sk_ref, wo_ref, fg_ref, y_ref):
    mix = (hn_ref[...] * og_ref[...] + sk_ref[...] * ca_ref[...]) * _silu(z_ref[...])
    h2 = h_ref[...] + _dot(mix.astype(BF16), wo_ref[...])
    ms = jnp.mean(h2 * h2, axis=-1, keepdims=True)
    y_ref[...] = h2 * lax.rsqrt(ms + EPS) * fg_ref[...]


def _tail1(hn, ca, z, h, og, sk, wo_bf, fg, grid, in_map, out_map, out_rows, tm):
    return pl.pallas_call(
        _tail1_kernel,
        grid=grid,
        in_specs=[pl.BlockSpec((tm, INNER_C), in_map)] * 3 + [pl.BlockSpec((tm, D_MODEL), in_map)]
                 + [_full(og.shape), _full(sk.shape), _full(wo_bf.shape), _full(fg.shape)],
        out_specs=pl.BlockSpec((tm, D_MODEL), out_map),
        out_shape=jax.ShapeDtypeStruct((out_rows, D_MODEL), F32),
        compiler_params=_cparams(*(["parallel"] * len(grid))),
        name="tail1",
    )(hn, ca, z, h, og, sk, wo_bf, fg)


def kernel(x_prompt, x_sample, cache_sb_k, cache_sb_v, page_table, state_pool, state_C, state_n, state_m, state_conv, meta_tokens, norm_g, final_norm_g, w_in_a, w_out_a, sb_bias, w_pool, pool_scale, w_in_c, b_gate_c, conv_w, conv_b, w_q, w_k, w_v, skip_c, outnorm_g, w_out_c):
    nb, seq, _ = x_prompt.shape
    bs, ts, _ = x_sample.shape
    lp = PAD + N_META + seq
    n_pool, page = cache_sb_k.shape[1], cache_sb_k.shape[2]
    tm = 2 * TILE

    r = jnp.arange(TILE)
    strict = (r[:, None] > r[None, :]).astype(BF16)
    uj1 = jnp.concatenate([strict, jnp.ones((TILE, TILE), BF16)], axis=1)
    uj = jnp.concatenate([uj1, uj1], axis=0)
    lower = (r[None, :] <= r[:, None]).astype(BF16)
    sel = (jnp.arange(16)[:, None] == r[None, :]).astype(BF16)

    w_in_a_bf = w_in_a[0].astype(BF16)
    w_out_a_bf = w_out_a[0].astype(BF16)
    w_pool_bf = w_pool[0].astype(BF16)
    ps = pool_scale[0].reshape(1, W_B)
    g0 = norm_g[0].reshape(1, D_MODEL)
    outs0 = [(0, W_A, DH_A ** -0.5, BF16),
             (W_A, W_A, 1.0, F32), (2 * W_A, W_A, 1.0, F32),
             (W_A, W_A, 1.0, BF16), (2 * W_A, W_A, 1.0, BF16),
             (3 * W_A, W_A, 1.0, F32),
             (4 * W_A, W_B, 1.0, F32), (4 * W_A + W_B, W_B, 1.0, F32)]

    meta = jnp.broadcast_to(meta_tokens.astype(x_prompt.dtype)[None], (nb, N_META, D_MODEL))
    h_p = jnp.concatenate([jnp.zeros((nb, PAD, D_MODEL), x_prompt.dtype), meta, x_prompt], axis=1)
    h_p = h_p.reshape(nb * lp, D_MODEL)
    q_bf, k_p, v_p, k_bf, v_bf, ga_p, u_p, gb_p = _norm_proj(h_p, g0, w_in_a_bf, outs0, tm)
    oa_p = _sb_prompt(sb_bias[0], q_bf, k_bf, v_bf, uj, nb, lp)
    h1_p = _tail0_prompt(oa_p, ga_p, u_p, gb_p, h_p, w_pool_bf, ps, w_out_a_bf, tm, lp)

    def to_tm(a):
        return a.transpose(1, 0, 2).reshape(ts * bs, a.shape[-1])

    def to_sm(a):
        return a.reshape(ts, bs, a.shape[-1]).transpose(1, 0, 2)

    h_s = to_tm(x_sample)
    outs0_s = [(0, W_A, DH_A ** -0.5, F32)] + outs0[1:3] + outs0[5:]
    q_s, k_s, v_s, ga_s, u_s, gb_s = _norm_proj(h_s, g0, w_in_a_bf, outs0_s, tm)
    q_rep = jnp.repeat(to_sm(q_s), H_A, axis=1)
    row_head = jnp.arange(ts * H_A) % H_A
    hm = (row_head[:, None] == (jnp.arange(W_A) // DH_A)[None, :]).astype(F32)
    bias_rows = jnp.broadcast_to(sb_bias[0][row_head][:, None], (ts * H_A, TILE))
    def pages_t(c):
        return c.transpose(0, 2, 3, 1).reshape(n_pool, W_A, page)

    oa_s = _sb_sample(page_table, q_rep, to_sm(k_s), to_sm(v_s), hm, bias_rows, uj,
                      pages_t(cache_sb_k[0]), pages_t(cache_sb_v[0]))
    pool_ext = jnp.concatenate([state_pool[0].transpose(1, 0, 2), u_s.reshape(ts, bs, W_B)], axis=0)
    h1_s = _tail0_sample(to_tm(oa_s), ga_s, pool_ext, gb_s, h_s, w_pool_bf, ps, w_out_a_bf)

    g1 = norm_g[1].reshape(1, D_MODEL)
    w_in_c_bf = w_in_c[0][:, :2 * INNER_C].astype(BF16)
    wg = jnp.zeros((D_MODEL, H_C, TILE), F32)
    wg = wg.at[:, :, 0].set(w_in_c[0][:, 2 * INNER_C:2 * INNER_C + H_C])
    wg = wg.at[:, :, 1].set(w_in_c[0][:, 2 * INNER_C + H_C:])
    wg_bf = wg.reshape(D_MODEL, H_C * TILE).astype(BF16)
    bg = jnp.zeros((H_C, TILE), F32).at[:, 0].set(b_gate_c[0][:H_C]).at[:, 1].set(b_gate_c[0][H_C:])
    bg = bg.reshape(1, H_C * TILE)
    cw, cb = conv_w[0], conv_b[0].reshape(1, INNER_C)
    wq_bf, wk_bf, wv_bf = w_q[0].astype(BF16), w_k[0].astype(BF16), w_v[0].astype(BF16)
    og, sk = outnorm_g[0].reshape(1, INNER_C), skip_c[0].reshape(1, INNER_C)
    w_out_c_bf = w_out_c[0].astype(BF16)
    fg = final_norm_g.reshape(1, D_MODEL)

    xm_p, z_p, gate_p = _norm_proj1(h1_p, g1, w_in_c_bf, wg_bf, bg, tm)
    ca_p, q1_p, k1_p, v1_p = _conv_qkv_prompt(xm_p, cw, cb, wq_bf, wk_bf, wv_bf, tm)
    hn_p, C_p, n_p, m_p = _mlstm_prompt(q1_p, k1_p, v1_p, gate_p, lower, sel, nb, lp)
    nc = lp // TILE
    y_prompt = _tail1(hn_p, ca_p, z_p, h1_p, og, sk, w_out_c_bf, fg, (nb, nc - 1),
                      lambda b, i: (b * nc + 1 + i, 0), lambda b, i: (b * (nc - 1) + i, 0), nb * seq, TILE)

    xm_s, z_s, gate_s = _norm_proj1(h1_s, g1, w_in_c_bf, wg_bf, bg, tm)
    conv_ext = jnp.concatenate([state_conv[0].transpose(1, 0, 2), xm_s.reshape(ts, bs, INNER_C)], axis=0)
    ca_s, q1_s, k1_s, v1_s = _conv_qkv_sample(conv_ext, cw, cb, wq_bf, wk_bf, wv_bf)
    rp = 16

    def to_rows(a):
        return jnp.pad(to_sm(a), ((0, 0), (0, rp - ts), (0, 0)))

    m0 = jnp.broadcast_to(state_m[0].reshape(bs * H_C, 1, 1), (bs * H_C, 1, TILE))
    hn_s, C_s, n_s, m_s = _mlstm_sample(to_rows(q1_s), to_rows(k1_s), to_rows(v1_s), to_rows(gate_s),
                                        state_C[0].reshape(bs * H_C, DH_C, DH_C),
                                        state_n[0].reshape(bs * H_C, 1, DH_C), m0, lower[:rp, :rp], sel, ts)
    y_s = _tail1(to_tm(hn_s[:, :ts]), ca_s, z_s, h1_s, og, sk, w_out_c_bf, fg, (ts * bs // tm,),
                 lambda i: (i, 0), lambda i: (i, 0), ts * bs, tm)

    def heads(a):
        return a.reshape(a.shape[:-1] + (H_A, DH_A))

    k3, v3, u3 = (a.reshape(nb, lp, -1) for a in (k_p, v_p, u_p))
    return (y_prompt.reshape(nb, seq, D_MODEL), to_sm(y_s),
            heads(k3[:, PAD:])[None], heads(v3[:, PAD:])[None],
            heads(to_sm(k_s))[None], heads(to_sm(v_s))[None],
            u3[:, lp - (POOL_MAX - 1):][None], pool_ext[ts:].transpose(1, 0, 2)[None],
            C_p.reshape(1, nb, H_C, DH_C, DH_C), C_s.reshape(1, bs, H_C, DH_C, DH_C),
            n_p.reshape(1, nb, H_C, DH_C), n_s.reshape(1, bs, H_C, DH_C),
            m_p[:, 0, 0].reshape(1, nb, H_C), m_s[:, 0, 0].reshape(1, bs, H_C),
            xm_p.reshape(nb, lp, INNER_C)[:, lp - (CONV_W - 1):][None], conv_ext[ts:].transpose(1, 0, 2)[None])
```

```python
import functools

import jax
import jax.numpy as jnp
from jax import lax
from jax.experimental import pallas as pl
from jax.experimental.pallas import tpu as pltpu

F32 = jnp.float32
BF16 = jnp.bfloat16

D_MODEL = 1024
N_META = 16
W_A = 512
H_A = 8
DH_A = 64
W_B = 512
POOL_WINDOWS = (2, 4, 8, 16)
POOL_GC = 128
POOL_MAX = 16
INNER_C = 2048
H_C = 4
DH_C = 512
CONV_W = 4
EPS = 1e-6

TILE = 128
PAD = TILE - N_META
NEG = -1e30
VMEM_LIMIT = 56 * 1024 * 1024


def _cparams(*sem):
    return pltpu.CompilerParams(dimension_semantics=sem, vmem_limit_bytes=VMEM_LIMIT)


def _silu(x):
    return x / (1.0 + jnp.exp(-x))


def _softplus(z):
    return jnp.maximum(z, 0.0) + jnp.log(1.0 + jnp.exp(-jnp.abs(z)))


def _split2(x):
    hi = x.astype(BF16)
    lo = (x - hi.astype(F32)).astype(BF16)
    return hi, lo


def _split3(x):
    hi = x.astype(BF16)
    r = x - hi.astype(F32)
    mid = r.astype(BF16)
    lo = (r - mid.astype(F32)).astype(BF16)
    return hi, mid, lo


def _dot(a, b):
    return jnp.dot(a, b, preferred_element_type=F32)


def _dot_nt(a, b):
    return lax.dot_general(a, b, (((1,), (1,)), ((), ())), preferred_element_type=F32)


def _dot_tn(a, b):
    return lax.dot_general(a, b, (((0,), (0,)), ((), ())), preferred_element_type=F32)


def _full(shape):
    nd = len(shape)
    return pl.BlockSpec(shape, lambda *_: (0,) * nd)


def _norm_proj_kernel(x_ref, g_ref, w_ref, *out_refs, outs):
    x = x_ref[...]
    ms = jnp.mean(x * x, axis=-1, keepdims=True)
    xn = (x * lax.rsqrt(ms + EPS) * g_ref[...]).astype(BF16)
    for o_ref, (off, width, scale) in zip(out_refs, outs):
        acc = _dot(xn, w_ref[:, off:off + width])
        if scale != 1.0:
            acc = acc * scale
        o_ref[...] = acc.astype(o_ref.dtype)


def _norm_proj(x, g, w_bf, outs, tm):
    rows = x.shape[0]
    kern = functools.partial(_norm_proj_kernel, outs=[(o, w, s) for o, w, s, _ in outs])
    return pl.pallas_call(
        kern,
        grid=(rows // tm,),
        in_specs=[pl.BlockSpec((tm, D_MODEL), lambda i: (i, 0)), _full((1, D_MODEL)), _full(w_bf.shape)],
        out_specs=[pl.BlockSpec((tm, w), lambda i: (i, 0)) for _, w, _, _ in outs],
        out_shape=[jax.ShapeDtypeStruct((rows, w), dt) for _, w, _, dt in outs],
        compiler_params=_cparams("parallel"),
        name="norm_proj",
    )(x, g, w_bf)


def _norm_proj1_kernel(x_ref, g_ref, w_ref, wg_ref, bg_ref, xm_ref, z_ref, gate_ref):
    x = x_ref[...]
    ms = jnp.mean(x * x, axis=-1, keepdims=True)
    xn = (x * lax.rsqrt(ms + EPS) * g_ref[...]).astype(BF16)
    xm_ref[...] = _dot(xn, w_ref[:, :INNER_C])
    z_ref[...] = _dot(xn, w_ref[:, INNER_C:])
    gl = _dot(xn, wg_ref[...]) + bg_ref[...]
    lane = lax.broadcasted_iota(jnp.int32, gl.shape, 1) % TILE
    gate_ref[...] = jnp.where(lane == 1, -_softplus(-gl), gl)


def _norm_proj1(x, g, w_bf, wg_bf, bg, tm):
    rows = x.shape[0]
    return pl.pallas_call(
        _norm_proj1_kernel,
        grid=(rows // tm,),
        in_specs=[pl.BlockSpec((tm, D_MODEL), lambda i: (i, 0)), _full((1, D_MODEL)), _full(w_bf.shape),
                  _full(wg_bf.shape), _full(bg.shape)],
        out_specs=[pl.BlockSpec((tm, INNER_C), lambda i: (i, 0)), pl.BlockSpec((tm, INNER_C), lambda i: (i, 0)),
                   pl.BlockSpec((tm, H_C * TILE), lambda i: (i, 0))],
        out_shape=[jax.ShapeDtypeStruct((rows, INNER_C), F32), jax.ShapeDtypeStruct((rows, INNER_C), F32),
                   jax.ShapeDtypeStruct((rows, H_C * TILE), F32)],
        compiler_params=_cparams("parallel"),
        name="norm_proj1",
    )(x, g, w_bf, wg_bf, bg)


LOG2E = 1.4426950408889634


def _softplus2(y):
    return jnp.maximum(y, 0.0) + jnp.log2(1.0 + jnp.exp2(-jnp.abs(y)))


def _sb_prompt_kernel(bias_ref, q_ref, k_ref, v_ref, uj_ref, o_ref, qm_ref):
    qi = pl.program_id(1)
    uj = uj_ref[...]
    lane = lax.broadcasted_iota(jnp.int32, (TILE, TILE), 1)
    first_half = lane < DH_A
    causal = lane < lax.broadcasted_iota(jnp.int32, (TILE, TILE), 0)
    for p in range(H_A // 2):
        qp = q_ref[:, p * TILE:(p + 1) * TILE]
        qm_ref[2 * p] = jnp.where(first_half, qp, jnp.zeros_like(qp))
        qm_ref[2 * p + 1] = jnp.where(first_half, jnp.zeros_like(qp), qp)

    def key_tiles(kss, mask, accs, cbs):
        heads = range(H_A)
        pairs = range(H_A // 2)
        tiles = range(len(kss))
        kps = [[k_ref[pl.ds(ks, TILE), p * TILE:(p + 1) * TILE] for p in pairs] for ks in kss]
        vps = [[v_ref[pl.ds(ks, TILE), p * TILE:(p + 1) * TILE] for p in pairs] for ks in kss]
        qps = [qm_ref[2 * p:2 * p + 2].reshape(2 * TILE, TILE) for p in pairs]
        zps = [[_dot_nt(qps[p], kps[i][p]) for p in pairs] for i in tiles]
        zs = [[zps[i][h // 2][(h % 2) * TILE:(h % 2 + 1) * TILE] + bias_ref[h] for h in heads] for i in tiles]
        sps = [[_softplus2(z) for z in zs[i]] for i in tiles]
        drops = [[sp if mask is None else jnp.where(mask, sp, 0.0) for sp in sps[i]] for i in tiles]
        tall = _dot(jnp.concatenate([jnp.concatenate(_split2(d), axis=1) for i in tiles for d in drops[i]],
                                    axis=0), uj)
        ts = [[tall[(i * H_A + h) * TILE:(i * H_A + h + 1) * TILE] for h in heads] for i in tiles]
        cbs = list(cbs)
        accs = list(accs)
        for i in tiles:
            ws = [jnp.exp2(zs[i][h] - sps[i][h] - ts[i][h][:, :TILE] - cbs[h]) for h in heads]
            if mask is not None:
                ws = [jnp.where(mask, a, 0.0) for a in ws]
            cbs = [cbs[h] + ts[i][h][:, TILE:] for h in heads]
            for p in pairs:
                r = _dot(jnp.concatenate([ws[2 * p].astype(BF16), ws[2 * p + 1].astype(BF16)], axis=0), vps[i][p])
                accs[p] = accs[p] + jnp.where(first_half, r[:TILE], r[TILE:])
        return tuple(accs), tuple(cbs)

    zero = jnp.zeros((TILE, TILE), F32)
    carry = key_tiles([pl.multiple_of(qi * TILE, TILE)], causal, (zero,) * (H_A // 2), (zero,) * H_A)

    def body(j, carry):
        ks = pl.multiple_of((qi - 1 - 2 * j) * TILE, TILE)
        return key_tiles([ks, pl.multiple_of(ks - TILE, TILE)], None, *carry)

    carry = lax.fori_loop(0, qi // 2, body, carry)
    accs, _ = lax.cond(qi % 2 == 1, lambda c: key_tiles([0], None, *c), lambda c: c, carry)
    for p in range(H_A // 2):
        o_ref[:, p * TILE:(p + 1) * TILE] = accs[p]


def _sb_prompt(bias, q_bf, k_bf, v_bf, uj, nb, lp):
    nq = lp // TILE
    return pl.pallas_call(
        _sb_prompt_kernel,
        grid=(nb, nq),
        in_specs=[pl.BlockSpec(memory_space=pltpu.SMEM),
                  pl.BlockSpec((TILE, W_A), lambda b, i: (b * nq + i, 0)),
                  pl.BlockSpec((lp, W_A), lambda b, i: (b, 0)),
                  pl.BlockSpec((lp, W_A), lambda b, i: (b, 0)),
                  _full(uj.shape)],
        out_specs=pl.BlockSpec((TILE, W_A), lambda b, i: (b * nq + i, 0)),
        out_shape=jax.ShapeDtypeStruct((nb * lp, W_A), F32),
        scratch_shapes=[pltpu.VMEM((H_A, TILE, TILE), BF16)],
        compiler_params=_cparams("parallel", "arbitrary"),
        name="sb_prompt",
    )(bias, q_bf, k_bf, v_bf, uj)


def _sb_sample_kernel(pt_ref, q_ref, kn_ref, vn_ref, hm_ref, bias_ref, uj_ref, *rest, n_pages, ts):
    kp_refs = rest[:n_pages]
    vp_refs = rest[n_pages:2 * n_pages]
    o_ref = rest[2 * n_pages]
    hm = hm_ref[...]
    qbd = q_ref[...] * hm
    bias = bias_ref[...]
    uj = uj_ref[...]
    rows = ts * H_A
    t_of_row = lax.broadcasted_iota(jnp.int32, (rows, 1), 0) // H_A
    out = jnp.zeros((rows, W_A), F32)
    carry = jnp.zeros((rows, 1), F32)
    for j in range(ts - 1, -1, -1):
        z = jnp.sum(qbd * kn_ref[j:j + 1, :], axis=-1, keepdims=True) + bias[:, 0:1]
        sp = _softplus2(z)
        m = j < t_of_row
        a = jnp.where(m, jnp.exp2(z - sp - carry), 0.0)
        carry = carry + jnp.where(m, sp, 0.0)
        out = out + a * vn_ref[j:j + 1, :]
    cb = jnp.broadcast_to(carry, (rows, TILE))
    qbd_bf = qbd.astype(BF16)
    pages = range(n_pages - 1, -1, -1)
    zs = [_dot(qbd_bf, kp_refs[p][...].astype(BF16)) + bias for p in pages]
    sps = [_softplus2(z) for z in zs]
    tall = _dot(jnp.concatenate([jnp.concatenate(_split2(sp), axis=1) for sp in sps], axis=0), uj)
    for i, p in enumerate(pages):
        tail = tall[i * rows:(i + 1) * rows]
        a = jnp.exp2(zs[i] - sps[i] - tail[:, :TILE] - cb)
        cb = cb + tail[:, TILE:]
        out = out + _dot_nt(a.astype(BF16), vp_refs[p][...].astype(BF16))
    o_ref[...] = jnp.sum((out * hm).reshape(ts, H_A, W_A), axis=1)


def _sb_sample(page_table, q_rep, kn, vn, hm, bias_rows, uj, cache_k, cache_v):
    bs, n_pages = page_table.shape
    ts = kn.shape[1]
    page = cache_k.shape[2]
    rows = ts * H_A

    def page_spec(p):
        return pl.BlockSpec((None, W_A, page), lambda s, pt: (pt[s * n_pages + p], 0, 0))

    grid_spec = pltpu.PrefetchScalarGridSpec(
        num_scalar_prefetch=1,
        grid=(bs,),
        in_specs=[pl.BlockSpec((None, rows, W_A), lambda s, pt: (s, 0, 0)),
                  pl.BlockSpec((None, ts, W_A), lambda s, pt: (s, 0, 0)),
                  pl.BlockSpec((None, ts, W_A), lambda s, pt: (s, 0, 0)),
                  pl.BlockSpec(hm.shape, lambda s, pt: (0, 0)),
                  pl.BlockSpec(bias_rows.shape, lambda s, pt: (0, 0)),
                  pl.BlockSpec(uj.shape, lambda s, pt: (0, 0))]
                 + [page_spec(p) for p in range(n_pages)] * 2,
        out_specs=pl.BlockSpec((None, ts, W_A), lambda s, pt: (s, 0, 0)),
    )
    return pl.pallas_call(
        functools.partial(_sb_sample_kernel, n_pages=n_pages, ts=ts),
        grid_spec=grid_spec,
        out_shape=jax.ShapeDtypeStruct((bs, ts, W_A), F32),
        compiler_params=_cparams("arbitrary"),
        name="sb_sample",
    )(page_table.reshape(-1), q_rep, kn, vn, hm, bias_rows, uj,
      *([cache_k] * n_pages), *([cache_v] * n_pages))


def _tail0_math(o_a, g_a, u, win_sums, cnts, g_b, h, wp_ref, ps, wo_ref):
    parts = [(o_a * _silu(g_a)).astype(BF16)]
    for g in range(len(POOL_WINDOWS)):
        cs = slice(g * POOL_GC, (g + 1) * POOL_GC)
        diff = win_sums[g] / cnts[g] - u[:, cs]
        y = _dot(diff.astype(BF16), wp_ref[g]) * ps[:, cs]
        parts.append((y * _silu(g_b[:, cs])).astype(BF16))
    mixed = jnp.concatenate(parts, axis=1)
    return h + _dot(mixed, wo_ref[...])


def _tail0_prompt_kernel(oa_ref, ga_ref, u_ref, halo_ref, gb_ref, h_ref, wp_ref, ps_ref, wo_ref, out_ref,
                         ext_ref, *, tm, lp):
    i = pl.program_id(0)
    ext_ref[0:POOL_MAX, :] = halo_ref[...]
    ext_ref[POOL_MAX:POOL_MAX + tm, :] = u_ref[...]
    row = (i * tm) % lp + lax.broadcasted_iota(jnp.int32, (tm, 1), 0)
    pos = jnp.where(row >= lp, row - lp, row) - PAD
    win_sums, cnts = [], []
    for g, w in enumerate(POOL_WINDOWS):
        cs = slice(g * POOL_GC, (g + 1) * POOL_GC)
        s = ext_ref[POOL_MAX:POOL_MAX + tm, cs]
        for j in range(1, w):
            s = s + ext_ref[POOL_MAX - j:POOL_MAX - j + tm, cs]
        win_sums.append(s)
        cnts.append(jnp.clip(pos + 1, 1, w).astype(F32))
    out_ref[...] = _tail0_math(oa_ref[...], ga_ref[...], u_ref[...], win_sums, cnts, gb_ref[...], h_ref[...],
                               wp_ref, ps_ref[...], wo_ref)


def _tail0_prompt(o_a, g_a, u, g_b, h, wp_bf, ps, wo_bf, tm, lp):
    rows = h.shape[0]
    hb = tm // POOL_MAX
    row = lambda i: (i, 0)
    return pl.pallas_call(
        functools.partial(_tail0_prompt_kernel, tm=tm, lp=lp),
        grid=(rows // tm,),
        in_specs=[pl.BlockSpec((tm, W_A), row), pl.BlockSpec((tm, W_A), row), pl.BlockSpec((tm, W_B), row),
                  pl.BlockSpec((POOL_MAX, W_B), lambda i: (jnp.maximum(i * hb - 1, 0), 0)),
                  pl.BlockSpec((tm, W_B), row), pl.BlockSpec((tm, D_MODEL), row),
                  _full(wp_bf.shape), _full(ps.shape), _full(wo_bf.shape)],
        out_specs=pl.BlockSpec((tm, D_MODEL), row),
        out_shape=jax.ShapeDtypeStruct((rows, D_MODEL), F32),
        scratch_shapes=[pltpu.VMEM((POOL_MAX + tm, W_B), F32)],
        compiler_params=_cparams("parallel"),
        name="tail0_prompt",
    )(o_a, g_a, u, u, g_b, h, wp_bf, ps, wo_bf)


def _tail0_sample_kernel(oa_ref, ga_ref, ext_ref, gb_ref, h_ref, wp_ref, ps_ref, wo_ref, out_ref, *, ts, bs):
    pre = POOL_MAX - 1
    u = jnp.concatenate([ext_ref[pre + t] for t in range(ts)], axis=0)
    win_sums, cnts = [], []
    for g, w in enumerate(POOL_WINDOWS):
        cs = slice(g * POOL_GC, (g + 1) * POOL_GC)
        per_t = []
        for t in range(ts):
            s = ext_ref[pre + t, :, cs]
            for j in range(1, w):
                s = s + ext_ref[pre + t - j, :, cs]
            per_t.append(s)
        win_sums.append(jnp.concatenate(per_t, axis=0))
        cnts.append(float(w))
    out_ref[...] = _tail0_math(oa_ref[...], ga_ref[...], u, win_sums, cnts, gb_ref[...], h_ref[...],
                               wp_ref, ps_ref[...], wo_ref)


def _tail0_sample(o_a, g_a, ext_t, g_b, h, wp_bf, ps, wo_bf):
    rows = h.shape[0]
    ts = ext_t.shape[0] - (POOL_MAX - 1)
    args = (o_a, g_a, ext_t, g_b, h, wp_bf, ps, wo_bf)
    return pl.pallas_call(
        functools.partial(_tail0_sample_kernel, ts=ts, bs=ext_t.shape[1]),
        grid=(1,),
        in_specs=[_full(a.shape) for a in args],
        out_specs=_full((rows, D_MODEL)),
        out_shape=jax.ShapeDtypeStruct((rows, D_MODEL), F32),
        compiler_params=_cparams("arbitrary"),
        name="tail0_sample",
    )(*args)


def _qkv_math(ca, xm, wq_ref, wk_ref, wv_ref, q_ref, k_ref, v_ref):
    ca_bf = ca.astype(BF16)
    xm_bf = xm.astype(BF16)
    for h in range(H_C):
        cs = slice(h * DH_C, (h + 1) * DH_C)
        q_ref[:, cs] = _dot(ca_bf[:, cs], wq_ref[h]).astype(BF16)
        k_ref[:, cs] = (_dot(ca_bf[:, cs], wk_ref[h]) * (DH_C ** -0.5)).astype(BF16)
        v_ref[:, cs] = _dot(xm_bf[:, cs], wv_ref[h]).astype(BF16)


def _conv_qkv_prompt_kernel(xm_ref, halo_ref, cw_ref, cb_ref, wq_ref, wk_ref, wv_ref,
                            ca_ref, q_ref, k_ref, v_ref, ext_ref, *, tm):
    ext_ref[0:8, :] = halo_ref[...]
    ext_ref[8:8 + tm, :] = xm_ref[...]
    conv = jnp.broadcast_to(cb_ref[...], (tm, INNER_C))
    for j in range(CONV_W):
        s = 8 - (CONV_W - 1) + j
        conv = conv + ext_ref[s:s + tm, :] * cw_ref[j:j + 1, :]
    ca = _silu(conv)
    ca_ref[...] = ca
    _qkv_math(ca, xm_ref[...], wq_ref, wk_ref, wv_ref, q_ref, k_ref, v_ref)


def _conv_qkv_prompt(xm, cw, cb, wq_bf, wk_bf, wv_bf, tm):
    rows = xm.shape[0]
    row = lambda i: (i, 0)
    return pl.pallas_call(
        functools.partial(_conv_qkv_prompt_kernel, tm=tm),
        grid=(rows // tm,),
        in_specs=[pl.BlockSpec((tm, INNER_C), row),
                  pl.BlockSpec((8, INNER_C), lambda i: (jnp.maximum(i * (tm // 8) - 1, 0), 0)),
                  _full(cw.shape), _full(cb.shape), _full(wq_bf.shape), _full(wk_bf.shape), _full(wv_bf.shape)],
        out_specs=[pl.BlockSpec((tm, INNER_C), row)] * 4,
        out_shape=[jax.ShapeDtypeStruct((rows, INNER_C), F32)] + [jax.ShapeDtypeStruct((rows, INNER_C), BF16)] * 3,
        scratch_shapes=[pltpu.VMEM((8 + tm, INNER_C), F32)],
        compiler_params=_cparams("parallel"),
        name="conv_qkv_prompt",
    )(xm, xm, cw, cb, wq_bf, wk_bf, wv_bf)


def _conv_qkv_sample_kernel(ext_ref, cw_ref, cb_ref, wq_ref, wk_ref, wv_ref, ca_ref, q_ref, k_ref, v_ref):
    t = pl.program_id(0)
    bs = ext_ref.shape[1]
    conv = jnp.broadcast_to(cb_ref[...], (bs, INNER_C))
    for j in range(CONV_W):
        conv = conv + ext_ref[t + j] * cw_ref[j:j + 1, :]
    ca = _silu(conv)
    ca_ref[...] = ca
    _qkv_math(ca, ext_ref[t + CONV_W - 1], wq_ref, wk_ref, wv_ref, q_ref, k_ref, v_ref)


def _conv_qkv_sample(ext_t, cw, cb, wq_bf, wk_bf, wv_bf):
    ts = ext_t.shape[0] - (CONV_W - 1)
    bs = ext_t.shape[1]
    row = lambda t: (t, 0)
    return pl.pallas_call(
        _conv_qkv_sample_kernel,
        grid=(ts,),
        in_specs=[_full(ext_t.shape), _full(cw.shape), _full(cb.shape), _full(wq_bf.shape), _full(wk_bf.shape),
                  _full(wv_bf.shape)],
        out_specs=[pl.BlockSpec((bs, INNER_C), row)] * 4,
        out_shape=[jax.ShapeDtypeStruct((ts * bs, INNER_C), F32)]
                  + [jax.ShapeDtypeStruct((ts * bs, INNER_C), BF16)] * 3,
        compiler_params=_cparams("arbitrary"),
        name="conv_qkv_sample",
    )(ext_t, cw, cb, wq_bf, wk_bf, wv_bf)


def _mlstm_heads(qs, ks, vs, gs, valid, Cs, ns, ms, lower, sel):
    H = range(len(qs))
    T = qs[0].shape[0]
    lane = lax.broadcasted_iota(jnp.int32, gs[0].shape, 1)
    pad_row = jnp.where(lane == 0, NEG, 0.0)
    gs = [jnp.where(valid, g, pad_row) for g in gs]
    qks = [_dot_nt(qs[h], ks[h]) for h in H]
    qcs = [_dot(qs[h], Cs[h].astype(BF16)) for h in H]
    g_parts = [_split3(g) for g in gs]
    cums = [sum(_dot(lower, p) for p in g_parts[h]) for h in H]
    rows_g = [sum(_dot_nt(sel, p) for p in g_parts[h]) for h in H]
    rows_c = [sum(_dot_nt(sel, p) for p in _split3(cums[h])) for h in H]
    tri = lax.broadcasted_iota(jnp.int32, (T, T), 1) <= lax.broadcasted_iota(jnp.int32, (T, T), 0)
    b_cs = [cums[h][:, 1:2] for h in H]
    ds = [jnp.where(tri, b_cs[h] - rows_c[h][1:2, :] + rows_g[h][0:1, :], NEG) for h in H]
    m_inters = [b_cs[h] + ms[h] for h in H]
    m_ts = [jnp.maximum(m_inters[h], jnp.max(ds[h], axis=-1, keepdims=True)) for h in H]
    ss = [jnp.where(tri, qks[h] * jnp.exp(ds[h] - m_ts[h]), 0.0) for h in H]
    svs = [_dot(ss[h].astype(BF16), vs[h]) for h in H]
    m_news = [m_ts[h][T - 1:T, :] for h in H]
    b_lasts = [b_cs[h][T - 1:T, :] for h in H]
    kws = [ks[h].astype(F32) * jnp.exp(b_lasts[h] - b_cs[h] + gs[h][:, 0:1] - m_news[h]) for h in H]
    upds = [_dot_tn(kws[h].astype(BF16), vs[h]) for h in H]
    decays = [jnp.exp(b_lasts[h] + ms[h] - m_news[h]) for h in H]
    C_news = [decays[h] * Cs[h] + upds[h] for h in H]
    n_news = [decays[h] * ns[h] + jnp.sum(kws[h], axis=0, keepdims=True) for h in H]
    hns = []
    for h in H:
        w_inter = jnp.exp(m_inters[h] - m_ts[h])
        num = w_inter * qcs[h] + svs[h]
        den = (w_inter * jnp.sum(qs[h].astype(F32) * ns[h], axis=-1, keepdims=True)
               + jnp.sum(ss[h], axis=-1, keepdims=True))
        hh = num / jnp.maximum(jnp.abs(den), jnp.exp(-m_ts[h]))
        hc = hh - jnp.mean(hh, axis=-1, keepdims=True)
        var = jnp.mean(hc * hc, axis=-1, keepdims=True)
        hns.append(hc * lax.rsqrt(var + EPS))
    return hns, C_news, n_news, m_news


def _mlstm_step(q_ref, k_ref, v_ref, g_ref, ca_ref, z_ref, og_ref, sk_ref, valid, c_in, n_in, m_in,
                lower_ref, sel_ref, mix_ref, c_ref, n_ref, m_ref):
    H = range(H_C)
    cols = [slice(h * DH_C, (h + 1) * DH_C) for h in H]
    hns, Cs, ns, ms = _mlstm_heads(
        [q_ref[:, c] for c in cols], [k_ref[:, c] for c in cols], [v_ref[:, c] for c in cols],
        [g_ref[:, h * TILE:(h + 1) * TILE] for h in H], valid,
        [c_in[h] for h in H], [n_in[h] for h in H], [m_in[h][:, 0:1] for h in H], lower_ref[...], sel_ref[...])
    for h in H:
        c = cols[h]
        mix = (hns[h] * og_ref[:, c] + sk_ref[:, c] * ca_ref[:, c]) * _silu(z_ref[:, c])
        mix_ref[:, c] = mix.astype(mix_ref.dtype)
        c_ref[h] = Cs[h]
        n_ref[h] = ns[h]
        m_ref[h] = jnp.broadcast_to(ms[h], (1, TILE))


def _mlstm_prompt_kernel(q_ref, k_ref, v_ref, g_ref, ca_ref, z_ref, og_ref, sk_ref, lower_ref, sel_ref,
                         mix_ref, c_ref, n_ref, m_ref):
    c = pl.program_id(1)

    @pl.when(c == 0)
    def _():
        c_ref[...] = jnp.zeros_like(c_ref)
        n_ref[...] = jnp.zeros_like(n_ref)
        m_ref[...] = jnp.zeros_like(m_ref)

    valid = c * TILE + lax.broadcasted_iota(jnp.int32, (TILE, 1), 0) >= PAD
    _mlstm_step(q_ref, k_ref, v_ref, g_ref, ca_ref, z_ref, og_ref, sk_ref, valid, c_ref, n_ref, m_ref,
                lower_ref, sel_ref, mix_ref, c_ref, n_ref, m_ref)


def _mlstm_prompt(q, k, v, g, ca, z, og, sk, lower, sel, nb, lp):
    nc = lp // TILE
    blk = lambda b, c: (b * nc + c, 0)
    st = lambda b, c: (b, 0, 0)
    return pl.pallas_call(
        _mlstm_prompt_kernel,
        grid=(nb, nc),
        in_specs=[pl.BlockSpec((TILE, INNER_C), blk)] * 3
                 + [pl.BlockSpec((TILE, H_C * TILE), blk), pl.BlockSpec((TILE, INNER_C), blk),
                    pl.BlockSpec((TILE, INNER_C), blk), _full(og.shape), _full(sk.shape),
                    _full(lower.shape), _full(sel.shape)],
        out_specs=[pl.BlockSpec((TILE, INNER_C), blk), pl.BlockSpec((H_C, DH_C, DH_C), st),
                   pl.BlockSpec((H_C, 1, DH_C), st), pl.BlockSpec((H_C, 1, TILE), st)],
        out_shape=[jax.ShapeDtypeStruct((nb * lp, INNER_C), BF16),
                   jax.ShapeDtypeStruct((nb * H_C, DH_C, DH_C), F32),
                   jax.ShapeDtypeStruct((nb * H_C, 1, DH_C), F32),
                   jax.ShapeDtypeStruct((nb * H_C, 1, TILE), F32)],
        compiler_params=_cparams("parallel", "arbitrary"),
        name="mlstm_prompt",
    )(q, k, v, g, ca, z, og, sk, lower, sel)


def _mlstm_sample_kernel(q_ref, k_ref, v_ref, g_ref, ca_ref, z_ref, og_ref, sk_ref, c_in, n_in, m_in,
                         lower_ref, sel_ref, mix_ref, c_ref, n_ref, m_ref, *, ts):
    valid = lax.broadcasted_iota(jnp.int32, (q_ref.shape[0], 1), 0) < ts
    _mlstm_step(q_ref, k_ref, v_ref, g_ref, ca_ref, z_ref, og_ref, sk_ref, valid, c_in, n_in, m_in,
                lower_ref, sel_ref, mix_ref, c_ref, n_ref, m_ref)


def _mlstm_sample(q, k, v, g, ca, z, og, sk, c0, n0, m0, lower, sel, ts):
    bs, rp = q.shape[0], q.shape[1]
    blk = lambda s: (s, 0, 0)
    rows = pl.BlockSpec((None, rp, INNER_C), blk)
    state_specs = [pl.BlockSpec((H_C, DH_C, DH_C), blk), pl.BlockSpec((H_C, 1, DH_C), blk),
                   pl.BlockSpec((H_C, 1, TILE), blk)]
    return pl.pallas_call(
        functools.partial(_mlstm_sample_kernel, ts=ts),
        grid=(bs,),
        in_specs=[rows] * 3 + [pl.BlockSpec((None, rp, H_C * TILE), blk), rows, rows,
                               _full(og.shape), _full(sk.shape)]
                 + state_specs + [_full(lower.shape), _full(sel.shape)],
        out_specs=[rows] + state_specs,
        out_shape=[jax.ShapeDtypeStruct((bs, rp, INNER_C), BF16),
                   jax.ShapeDtypeStruct((bs * H_C, DH_C, DH_C), F32),
                   jax.ShapeDtypeStruct((bs * H_C, 1, DH_C), F32),
                   jax.ShapeDtypeStruct((bs * H_C, 1, TILE), F32)],
        compiler_params=_cparams("parallel"),
        name="mlstm_sample",
    )(q, k, v, g, ca, z, og, sk, c0, n0, m0, lower, sel)


def _tail1_kernel(mix_ref, h_ref, wo_ref, fg_ref, y_ref):
    h2 = h_ref[...] + _dot(mix_ref[...], wo_ref[...])
    ms = jnp.mean(h2 * h2, axis=-1, keepdims=True)
    y_ref[...] = h2 * lax.rsqrt(ms + EPS) * fg_ref[...]


def _tail1(mix, h, wo_bf, fg, grid, in_map, out_map, out_rows, tm):
    return pl.pallas_call(
        _tail1_kernel,
        grid=grid,
        in_specs=[pl.BlockSpec((tm, INNER_C), in_map), pl.BlockSpec((tm, D_MODEL), in_map),
                  _full(wo_bf.shape), _full(fg.shape)],
        out_specs=pl.BlockSpec((tm, D_MODEL), out_map),
        out_shape=jax.ShapeDtypeStruct((out_rows, D_MODEL), F32),
        compiler_params=_cparams(*(["parallel"] * len(grid))),
        name="tail1",
    )(mix, h, wo_bf, fg)


def kernel(x_prompt, x_sample, cache_sb_k, cache_sb_v, page_table, state_pool, state_C, state_n, state_m, state_conv, meta_tokens, norm_g, final_norm_g, w_in_a, w_out_a, sb_bias, w_pool, pool_scale, w_in_c, b_gate_c, conv_w, conv_b, w_q, w_k, w_v, skip_c, outnorm_g, w_out_c):
    nb, seq, _ = x_prompt.shape
    bs, ts, _ = x_sample.shape
    lp = PAD + N_META + seq
    n_pool, page = cache_sb_k.shape[1], cache_sb_k.shape[2]
    tm = 2 * TILE

    r = jnp.arange(TILE)
    strict = (r[:, None] > r[None, :]).astype(BF16)
    uj1 = jnp.concatenate([strict, jnp.ones((TILE, TILE), BF16)], axis=1)
    uj = jnp.concatenate([uj1, uj1], axis=0)
    lower = (r[None, :] <= r[:, None]).astype(BF16)
    sel = (jnp.arange(16)[:, None] == r[None, :]).astype(BF16)

    w_in_a_bf = w_in_a[0].astype(BF16)
    w_out_a_bf = w_out_a[0].astype(BF16)
    w_pool_bf = w_pool[0].astype(BF16)
    ps = pool_scale[0].reshape(1, W_B)
    g0 = norm_g[0].reshape(1, D_MODEL)
    q_scale = DH_A ** -0.5 * LOG2E
    bias2 = sb_bias[0] * LOG2E
    outs0 = [(0, W_A, q_scale, BF16),
             (W_A, W_A, 1.0, F32), (2 * W_A, W_A, 1.0, F32),
             (W_A, W_A, 1.0, BF16), (2 * W_A, W_A, 1.0, BF16),
             (3 * W_A, W_A, 1.0, F32),
             (4 * W_A, W_B, 1.0, F32), (4 * W_A + W_B, W_B, 1.0, F32)]

    meta = jnp.broadcast_to(meta_tokens.astype(x_prompt.dtype)[None], (nb, N_META, D_MODEL))
    h_p = jnp.concatenate([jnp.zeros((nb, PAD, D_MODEL), x_prompt.dtype), meta, x_prompt], axis=1)
    h_p = h_p.reshape(nb * lp, D_MODEL)
    q_bf, k_p, v_p, k_bf, v_bf, ga_p, u_p, gb_p = _norm_proj(h_p, g0, w_in_a_bf, outs0, tm)
    oa_p = _sb_prompt(bias2, q_bf, k_bf, v_bf, uj, nb, lp)
    h1_p = _tail0_prompt(oa_p, ga_p, u_p, gb_p, h_p, w_pool_bf, ps, w_out_a_bf, tm, lp)

    def to_tm(a):
        return a.transpose(1, 0, 2).reshape(ts * bs, a.shape[-1])

    def to_sm(a):
        return a.reshape(ts, bs, a.shape[-1]).transpose(1, 0, 2)

    h_s = to_tm(x_sample)
    outs0_s = [(0, W_A, q_scale, F32)] + outs0[1:3] + outs0[5:]
    q_s, k_s, v_s, ga_s, u_s, gb_s = _norm_proj(h_s, g0, w_in_a_bf, outs0_s, tm)
    q_rep = jnp.repeat(to_sm(q_s), H_A, axis=1)
    row_head = jnp.arange(ts * H_A) % H_A
    hm = (row_head[:, None] == (jnp.arange(W_A) // DH_A)[None, :]).astype(F32)
    bias_rows = jnp.broadcast_to(bias2[row_head][:, None], (ts * H_A, TILE))
    def pages_t(c):
        return c.transpose(0, 2, 3, 1).reshape(n_pool, W_A, page)

    oa_s = _sb_sample(page_table, q_rep, to_sm(k_s), to_sm(v_s), hm, bias_rows, uj,
                      pages_t(cache_sb_k[0]), pages_t(cache_sb_v[0]))
    pool_ext = jnp.concatenate([state_pool[0].transpose(1, 0, 2), u_s.reshape(ts, bs, W_B)], axis=0)
    h1_s = _tail0_sample(to_tm(oa_s), ga_s, pool_ext, gb_s, h_s, w_pool_bf, ps, w_out_a_bf)

    g1 = norm_g[1].reshape(1, D_MODEL)
    w_in_c_bf = w_in_c[0][:, :2 * INNER_C].astype(BF16)
    wg = jnp.zeros((D_MODEL, H_C, TILE), F32)
    wg = wg.at[:, :, 0].set(w_in_c[0][:, 2 * INNER_C:2 * INNER_C + H_C])
    wg = wg.at[:, :, 1].set(w_in_c[0][:, 2 * INNER_C + H_C:])
    wg_bf = wg.reshape(D_MODEL, H_C * TILE).astype(BF16)
    bg = jnp.zeros((H_C, TILE), F32).at[:, 0].set(b_gate_c[0][:H_C]).at[:, 1].set(b_gate_c[0][H_C:])
    bg = bg.reshape(1, H_C * TILE)
    cw, cb = conv_w[0], conv_b[0].reshape(1, INNER_C)
    wq_bf, wk_bf, wv_bf = w_q[0].astype(BF16), w_k[0].astype(BF16), w_v[0].astype(BF16)
    og, sk = outnorm_g[0].reshape(1, INNER_C), skip_c[0].reshape(1, INNER_C)
    w_out_c_bf = w_out_c[0].astype(BF16)
    fg = final_norm_g.reshape(1, D_MODEL)

    xm_p, z_p, gate_p = _norm_proj1(h1_p, g1, w_in_c_bf, wg_bf, bg, tm)
    ca_p, q1_p, k1_p, v1_p = _conv_qkv_prompt(xm_p, cw, cb, wq_bf, wk_bf, wv_bf, tm)
    mix_p, C_p, n_p, m_p = _mlstm_prompt(q1_p, k1_p, v1_p, gate_p, ca_p, z_p, og, sk, lower, sel, nb, lp)
    nc = lp // TILE
    y_prompt = _tail1(mix_p, h1_p, w_out_c_bf, fg, (nb, nc - 1),
                      lambda b, i: (b * nc + 1 + i, 0), lambda b, i: (b * (nc - 1) + i, 0), nb * seq, TILE)

    xm_s, z_s, gate_s = _norm_proj1(h1_s, g1, w_in_c_bf, wg_bf, bg, tm)
    conv_ext = jnp.concatenate([state_conv[0].transpose(1, 0, 2), xm_s.reshape(ts, bs, INNER_C)], axis=0)
    ca_s, q1_s, k1_s, v1_s = _conv_qkv_sample(conv_ext, cw, cb, wq_bf, wk_bf, wv_bf)
    rp = 16

    def to_rows(a):
        return jnp.pad(to_sm(a), ((0, 0), (0, rp - ts), (0, 0)))

    m0 = jnp.broadcast_to(state_m[0].reshape(bs * H_C, 1, 1), (bs * H_C, 1, TILE))
    mix_s, C_s, n_s, m_s = _mlstm_sample(to_rows(q1_s), to_rows(k1_s), to_rows(v1_s), to_rows(gate_s),
                                         to_rows(ca_s), to_rows(z_s), og, sk,
                                         state_C[0].reshape(bs * H_C, DH_C, DH_C),
                                         state_n[0].reshape(bs * H_C, 1, DH_C), m0, lower[:rp, :rp], sel, ts)
    y_s = _tail1(to_tm(mix_s[:, :ts]), h1_s, w_out_c_bf, fg, (ts * bs // tm,),
                 lambda i: (i, 0), lambda i: (i, 0), ts * bs, tm)

    def heads(a):
        return a.reshape(a.shape[:-1] + (H_A, DH_A))

    k3, v3, u3 = (a.reshape(nb, lp, -1) for a in (k_p, v_p, u_p))
    return (y_prompt.reshape(nb, seq, D_MODEL), to_sm(y_s),
            heads(k3[:, PAD:])[None], heads(v3[:, PAD:])[None],
            heads(to_sm(k_s))[None], heads(to_sm(v_s))[None],
            u3[:, lp - (POOL_MAX - 1):][None], pool_ext[ts:].transpose(1, 0, 2)[None],
            C_p.reshape(1, nb, H_C, DH_C, DH_C), C_s.reshape(1, bs, H_C, DH_C, DH_C),
            n_p.reshape(1, nb, H_C, DH_C), n_s.reshape(1, bs, H_C, DH_C),
            m_p[:, 0, 0].reshape(1, nb, H_C), m_s[:, 0, 0].reshape(1, bs, H_C),
            xm_p.reshape(nb, lp, INNER_C)[:, lp - (CONV_W - 1):][None], conv_ext[ts:].transpose(1, 0, 2)[None])
```

```python
import functools

import jax
import jax.numpy as jnp
from jax import lax
from jax.experimental import pallas as pl
from jax.experimental.pallas import tpu as pltpu

F32 = jnp.float32
BF16 = jnp.bfloat16

D_MODEL = 1024
N_META = 16
W_A = 512
H_A = 8
DH_A = 64
W_B = 512
POOL_WINDOWS = (2, 4, 8, 16)
POOL_GC = 128
POOL_MAX = 16
INNER_C = 2048
H_C = 4
DH_C = 512
CONV_W = 4
EPS = 1e-6

TILE = 128
PAD = TILE - N_META
NEG = -1e30
VMEM_LIMIT = 56 * 1024 * 1024


def _cparams(*sem):
    return pltpu.CompilerParams(dimension_semantics=sem, vmem_limit_bytes=VMEM_LIMIT)


def _silu(x):
    return x / (1.0 + jnp.exp(-x))


def _softplus(z):
    return jnp.maximum(z, 0.0) + jnp.log(1.0 + jnp.exp(-jnp.abs(z)))


def _split2(x):
    hi = x.astype(BF16)
    lo = (x - hi.astype(F32)).astype(BF16)
    return hi, lo


def _split3(x):
    hi = x.astype(BF16)
    r = x - hi.astype(F32)
    mid = r.astype(BF16)
    lo = (r - mid.astype(F32)).astype(BF16)
    return hi, mid, lo


def _dot(a, b):
    return jnp.dot(a, b, preferred_element_type=F32)


def _dot_nt(a, b):
    return lax.dot_general(a, b, (((1,), (1,)), ((), ())), preferred_element_type=F32)


def _dot_tn(a, b):
    return lax.dot_general(a, b, (((0,), (0,)), ((), ())), preferred_element_type=F32)


def _full(shape):
    nd = len(shape)
    return pl.BlockSpec(shape, lambda *_: (0,) * nd)


def _norm_proj_kernel(x_ref, g_ref, w_ref, *out_refs, outs):
    x = x_ref[...]
    ms = jnp.mean(x * x, axis=-1, keepdims=True)
    xn = (x * lax.rsqrt(ms + EPS) * g_ref[...]).astype(BF16)
    accs = {}
    for o_ref, (off, width, scale) in zip(out_refs, outs):
        if (off, width) not in accs:
            accs[(off, width)] = _dot(xn, w_ref[:, off:off + width])
        acc = accs[(off, width)]
        if scale != 1.0:
            acc = acc * scale
        o_ref[...] = acc.astype(o_ref.dtype)


def _norm_proj(x, g, w_bf, outs, tm):
    rows = x.shape[0]
    kern = functools.partial(_norm_proj_kernel, outs=[(o, w, s) for o, w, s, _ in outs])
    return pl.pallas_call(
        kern,
        grid=(rows // tm,),
        in_specs=[pl.BlockSpec((tm, D_MODEL), lambda i: (i, 0)), _full((1, D_MODEL)), _full(w_bf.shape)],
        out_specs=[pl.BlockSpec((tm, w), lambda i: (i, 0)) for _, w, _, _ in outs],
        out_shape=[jax.ShapeDtypeStruct((rows, w), dt) for _, w, _, dt in outs],
        compiler_params=_cparams("parallel"),
        name="norm_proj",
    )(x, g, w_bf)


def _norm_proj1_kernel(x_ref, g_ref, w_ref, wg_ref, bg_ref, xm_ref, z_ref, gate_ref):
    x = x_ref[...]
    ms = jnp.mean(x * x, axis=-1, keepdims=True)
    xn = (x * lax.rsqrt(ms + EPS) * g_ref[...]).astype(BF16)
    xm_ref[...] = _dot(xn, w_ref[:, :INNER_C])
    z_ref[...] = _dot(xn, w_ref[:, INNER_C:])
    gl = _dot(xn, wg_ref[...]) + bg_ref[...]
    lane = lax.broadcasted_iota(jnp.int32, gl.shape, 1)
    gate_ref[...] = jnp.where(lane >= H_C, -_softplus(-gl), gl)


def _norm_proj1(x, g, w_bf, wg_bf, bg, tm):
    rows = x.shape[0]
    return pl.pallas_call(
        _norm_proj1_kernel,
        grid=(rows // tm,),
        in_specs=[pl.BlockSpec((tm, D_MODEL), lambda i: (i, 0)), _full((1, D_MODEL)), _full(w_bf.shape),
                  _full(wg_bf.shape), _full(bg.shape)],
        out_specs=[pl.BlockSpec((tm, INNER_C), lambda i: (i, 0)), pl.BlockSpec((tm, INNER_C), lambda i: (i, 0)),
                   pl.BlockSpec((tm, TILE), lambda i: (i, 0))],
        out_shape=[jax.ShapeDtypeStruct((rows, INNER_C), F32), jax.ShapeDtypeStruct((rows, INNER_C), F32),
                   jax.ShapeDtypeStruct((rows, TILE), F32)],
        compiler_params=_cparams("parallel"),
        name="norm_proj1",
    )(x, g, w_bf, wg_bf, bg)


LOG2E = 1.4426950408889634


def _softplus2(y):
    return jnp.where(y > 64.0, y, jnp.log2(1.0 + jnp.exp2(y)))


def _sb_prompt_kernel(bias_ref, q_ref, k_ref, v_ref, uinc_ref, o_ref, qm_ref):
    qi = pl.program_id(1)
    uinc = uinc_ref[...]
    lane = lax.broadcasted_iota(jnp.int32, (TILE, TILE), 1)
    first_half = lane < DH_A
    causal = lane < lax.broadcasted_iota(jnp.int32, (TILE, TILE), 0)
    heads = range(H_A)
    pairs = range(H_A // 2)
    for p in pairs:
        qp = q_ref[:, p * TILE:(p + 1) * TILE]
        qm_ref[2 * p] = jnp.where(first_half, qp, jnp.zeros_like(qp))
        qm_ref[2 * p + 1] = jnp.where(first_half, jnp.zeros_like(qp), qp)

    def key_tiles(kss, mask, accs, cbs):
        tiles = range(len(kss))
        kps = [[k_ref[pl.ds(ks, TILE), p * TILE:(p + 1) * TILE] for p in pairs] for ks in kss]
        vps = [[v_ref[pl.ds(ks, TILE), p * TILE:(p + 1) * TILE] for p in pairs] for ks in kss]
        qps = [qm_ref[2 * p:2 * p + 2].reshape(2 * TILE, TILE) for p in pairs]
        zps = [[_dot_nt(qps[p], kps[i][p]) for p in pairs] for i in tiles]
        zs, ts = [], []
        group = H_A // 2
        for i in tiles:
            zs.append([zps[i][h // 2][(h % 2) * TILE:(h % 2 + 1) * TILE] + bias_ref[h] for h in heads])
            ts.append([])
            for g0 in range(0, H_A, group):
                drops = [_softplus2(z) for z in zs[i][g0:g0 + group]]
                if mask is not None:
                    drops = [jnp.where(mask, d, 0.0) for d in drops]
                tall = _dot(jnp.concatenate([jnp.concatenate(_split2(d), axis=1) for d in drops], axis=0), uinc)
                ts[i] += [tall[h * TILE:(h + 1) * TILE] for h in range(group)]
        cbs = list(cbs)
        accs = list(accs)
        for i in tiles:
            ws = [jnp.exp2(zs[i][h] - ts[i][h] - cbs[h]) for h in heads]
            if mask is not None:
                ws = [jnp.where(mask, a, 0.0) for a in ws]
            cbs = [cbs[h] + jnp.broadcast_to(ts[i][h][:, 0:1], (TILE, TILE)) for h in heads]
            for p in pairs:
                r = _dot(jnp.concatenate([ws[2 * p].astype(BF16), ws[2 * p + 1].astype(BF16)], axis=0), vps[i][p])
                accs[p] = accs[p] + jnp.where(first_half, r[:TILE], r[TILE:])
        return tuple(accs), tuple(cbs)

    zero = jnp.zeros((TILE, TILE), F32)
    carry = key_tiles([pl.multiple_of(qi * TILE, TILE)], causal, (zero,) * (H_A // 2), (zero,) * H_A)

    def body(j, carry):
        ks = pl.multiple_of((qi - 1 - 2 * j) * TILE, TILE)
        return key_tiles([ks, pl.multiple_of(ks - TILE, TILE)], None, *carry)

    carry = lax.fori_loop(0, qi // 2, body, carry)
    accs, _ = lax.cond(qi % 2 == 1, lambda c: key_tiles([0], None, *c), lambda c: c, carry)
    for p in pairs:
        o_ref[:, p * TILE:(p + 1) * TILE] = accs[p]


def _sb_prompt(bias, q_bf, k_bf, v_bf, uinc, nb, lp):
    nq = lp // TILE
    return pl.pallas_call(
        _sb_prompt_kernel,
        grid=(nb, nq),
        in_specs=[pl.BlockSpec(memory_space=pltpu.SMEM),
                  pl.BlockSpec((TILE, W_A), lambda b, i: (b * nq + i, 0)),
                  pl.BlockSpec((lp, W_A), lambda b, i: (b, 0)),
                  pl.BlockSpec((lp, W_A), lambda b, i: (b, 0)),
                  _full(uinc.shape)],
        out_specs=pl.BlockSpec((TILE, W_A), lambda b, i: (b * nq + i, 0)),
        out_shape=jax.ShapeDtypeStruct((nb * lp, W_A), F32),
        scratch_shapes=[pltpu.VMEM((H_A, TILE, TILE), BF16)],
        compiler_params=_cparams("parallel", "arbitrary"),
        name="sb_prompt",
    )(bias, q_bf, k_bf, v_bf, uinc)


def _sb_sample_kernel(pt_ref, q_ref, kn_ref, vn_ref, hm_ref, bias_ref, uinc_ref, *rest, n_pages, ts):
    kp_refs = rest[:n_pages]
    vp_refs = rest[n_pages:2 * n_pages]
    o_ref = rest[2 * n_pages]
    hm = hm_ref[...]
    qbd = q_ref[...] * hm
    bias = bias_ref[...]
    uinc = uinc_ref[...]
    rows = ts * H_A
    t_of_row = lax.broadcasted_iota(jnp.int32, (rows, 1), 0) // H_A
    out = jnp.zeros((rows, W_A), F32)
    carry = jnp.zeros((rows, 1), F32)
    for j in range(ts - 1, -1, -1):
        z = jnp.sum(qbd * kn_ref[j:j + 1, :], axis=-1, keepdims=True) + bias[:, 0:1]
        sp = _softplus2(z)
        m = j < t_of_row
        a = jnp.where(m, jnp.exp2(z - sp - carry), 0.0)
        carry = carry + jnp.where(m, sp, 0.0)
        out = out + a * vn_ref[j:j + 1, :]
    cb = jnp.broadcast_to(carry, (rows, TILE))
    qbd_bf = qbd.astype(BF16)
    pages = range(n_pages - 1, -1, -1)
    zs = [_dot(qbd_bf, kp_refs[p][...].astype(BF16)) + bias for p in pages]
    sps = [_softplus2(z) for z in zs]
    tall = _dot(jnp.concatenate([jnp.concatenate(_split2(sp), axis=1) for sp in sps], axis=0), uinc)
    for i, p in enumerate(pages):
        tail = tall[i * rows:(i + 1) * rows]
        a = jnp.exp2(zs[i] - tail - cb)
        cb = cb + jnp.broadcast_to(tail[:, 0:1], (rows, TILE))
        out = out + _dot_nt(a.astype(BF16), vp_refs[p][...].astype(BF16))
    o_ref[...] = jnp.sum((out * hm).reshape(ts, H_A, W_A), axis=1)


def _sb_sample(page_table, q_rep, kn, vn, hm, bias_rows, uinc, cache_k, cache_v):
    bs, n_pages = page_table.shape
    ts = kn.shape[1]
    page = cache_k.shape[2]
    rows = ts * H_A

    def page_spec(p):
        return pl.BlockSpec((None, W_A, page), lambda s, pt: (pt[s * n_pages + p], 0, 0))

    grid_spec = pltpu.PrefetchScalarGridSpec(
        num_scalar_prefetch=1,
        grid=(bs,),
        in_specs=[pl.BlockSpec((None, rows, W_A), lambda s, pt: (s, 0, 0)),
                  pl.BlockSpec((None, ts, W_A), lambda s, pt: (s, 0, 0)),
                  pl.BlockSpec((None, ts, W_A), lambda s, pt: (s, 0, 0)),
                  pl.BlockSpec(hm.shape, lambda s, pt: (0, 0)),
                  pl.BlockSpec(bias_rows.shape, lambda s, pt: (0, 0)),
                  pl.BlockSpec(uinc.shape, lambda s, pt: (0, 0))]
                 + [page_spec(p) for p in range(n_pages)] * 2,
        out_specs=pl.BlockSpec((None, ts, W_A), lambda s, pt: (s, 0, 0)),
    )
    return pl.pallas_call(
        functools.partial(_sb_sample_kernel, n_pages=n_pages, ts=ts),
        grid_spec=grid_spec,
        out_shape=jax.ShapeDtypeStruct((bs, ts, W_A), F32),
        compiler_params=_cparams("arbitrary"),
        name="sb_sample",
    )(page_table.reshape(-1), q_rep, kn, vn, hm, bias_rows, uinc,
      *([cache_k] * n_pages), *([cache_v] * n_pages))


def _tail0_math(o_a, g_a, u, win_sums, cnts, g_b, h, wp_ref, ps, wo_ref):
    parts = [(o_a * _silu(g_a)).astype(BF16)]
    for g in range(len(POOL_WINDOWS)):
        cs = slice(g * POOL_GC, (g + 1) * POOL_GC)
        diff = win_sums[g] / cnts[g] - u[:, cs]
        y = _dot(diff.astype(BF16), wp_ref[g]) * ps[:, cs]
        parts.append((y * _silu(g_b[:, cs])).astype(BF16))
    mixed = jnp.concatenate(parts, axis=1)
    return h + _dot(mixed, wo_ref[...])


def _tail0_prompt_kernel(oa_ref, ga_ref, u_ref, halo_ref, gb_ref, h_ref, wp_ref, ps_ref, wo_ref, out_ref,
                         ext_ref, *, tm, lp):
    i = pl.program_id(0)
    ext_ref[0:POOL_MAX, :] = halo_ref[...]
    ext_ref[POOL_MAX:POOL_MAX + tm, :] = u_ref[...]
    row = (i * tm) % lp + lax.broadcasted_iota(jnp.int32, (tm, 1), 0)
    pos = jnp.where(row >= lp, row - lp, row) - PAD
    win_sums, cnts = [], []
    for g, w in enumerate(POOL_WINDOWS):
        cs = slice(g * POOL_GC, (g + 1) * POOL_GC)
        s = ext_ref[POOL_MAX:POOL_MAX + tm, cs]
        for j in range(1, w):
            s = s + ext_ref[POOL_MAX - j:POOL_MAX - j + tm, cs]
        win_sums.append(s)
        cnts.append(jnp.clip(pos + 1, 1, w).astype(F32))
    out_ref[...] = _tail0_math(oa_ref[...], ga_ref[...], u_ref[...], win_sums, cnts, gb_ref[...], h_ref[...],
                               wp_ref, ps_ref[...], wo_ref)


def _tail0_prompt(o_a, g_a, u, g_b, h, wp_bf, ps, wo_bf, tm, lp):
    rows = h.shape[0]
    hb = tm // POOL_MAX
    row = lambda i: (i, 0)
    return pl.pallas_call(
        functools.partial(_tail0_prompt_kernel, tm=tm, lp=lp),
        grid=(rows // tm,),
        in_specs=[pl.BlockSpec((tm, W_A), row), pl.BlockSpec((tm, W_A), row), pl.BlockSpec((tm, W_B), row),
                  pl.BlockSpec((POOL_MAX, W_B), lambda i: (jnp.maximum(i * hb - 1, 0), 0)),
                  pl.BlockSpec((tm, W_B), row), pl.BlockSpec((tm, D_MODEL), row),
                  _full(wp_bf.shape), _full(ps.shape), _full(wo_bf.shape)],
        out_specs=pl.BlockSpec((tm, D_MODEL), row),
        out_shape=jax.ShapeDtypeStruct((rows, D_MODEL), F32),
        scratch_shapes=[pltpu.VMEM((POOL_MAX + tm, W_B), F32)],
        compiler_params=_cparams("parallel"),
        name="tail0_prompt",
    )(o_a, g_a, u, u, g_b, h, wp_bf, ps, wo_bf)


def _tail0_sample_kernel(oa_ref, ga_ref, ext_ref, gb_ref, h_ref, wp_ref, ps_ref, wo_ref, out_ref, *, ts, bs):
    pre = POOL_MAX - 1
    u = jnp.concatenate([ext_ref[pre + t] for t in range(ts)], axis=0)
    win_sums, cnts = [], []
    for g, w in enumerate(POOL_WINDOWS):
        cs = slice(g * POOL_GC, (g + 1) * POOL_GC)
        per_t = []
        for t in range(ts):
            s = ext_ref[pre + t, :, cs]
            for j in range(1, w):
                s = s + ext_ref[pre + t - j, :, cs]
            per_t.append(s)
        win_sums.append(jnp.concatenate(per_t, axis=0))
        cnts.append(float(w))
    out_ref[...] = _tail0_math(oa_ref[...], ga_ref[...], u, win_sums, cnts, gb_ref[...], h_ref[...],
                               wp_ref, ps_ref[...], wo_ref)


def _tail0_sample(o_a, g_a, ext_t, g_b, h, wp_bf, ps, wo_bf):
    rows = h.shape[0]
    ts = ext_t.shape[0] - (POOL_MAX - 1)
    args = (o_a, g_a, ext_t, g_b, h, wp_bf, ps, wo_bf)
    return pl.pallas_call(
        functools.partial(_tail0_sample_kernel, ts=ts, bs=ext_t.shape[1]),
        grid=(1,),
        in_specs=[_full(a.shape) for a in args],
        out_specs=_full((rows, D_MODEL)),
        out_shape=jax.ShapeDtypeStruct((rows, D_MODEL), F32),
        compiler_params=_cparams("arbitrary"),
        name="tail0_sample",
    )(*args)


def _qkv_math(ca, xm, wq_ref, wk_ref, wv_ref, q_ref, k_ref, v_ref):
    ca_bf = ca.astype(BF16)
    xm_bf = xm.astype(BF16)
    for h in range(H_C):
        cs = slice(h * DH_C, (h + 1) * DH_C)
        q_ref[:, cs] = _dot(ca_bf[:, cs], wq_ref[h]).astype(BF16)
        k_ref[:, cs] = (_dot(ca_bf[:, cs], wk_ref[h]) * (DH_C ** -0.5)).astype(BF16)
        v_ref[:, cs] = _dot(xm_bf[:, cs], wv_ref[h]).astype(BF16)


def _conv_qkv_prompt_kernel(xm_ref, halo_ref, cw_ref, cb_ref, wq_ref, wk_ref, wv_ref,
                            ca_ref, q_ref, k_ref, v_ref, ext_ref, *, tm):
    ext_ref[0:8, :] = halo_ref[...]
    ext_ref[8:8 + tm, :] = xm_ref[...]
    conv = jnp.broadcast_to(cb_ref[...], (tm, INNER_C))
    for j in range(CONV_W):
        s = 8 - (CONV_W - 1) + j
        conv = conv + ext_ref[s:s + tm, :] * cw_ref[j:j + 1, :]
    ca = _silu(conv)
    ca_ref[...] = ca
    _qkv_math(ca, xm_ref[...], wq_ref, wk_ref, wv_ref, q_ref, k_ref, v_ref)


def _conv_qkv_prompt(xm, cw, cb, wq_bf, wk_bf, wv_bf, tm):
    rows = xm.shape[0]
    row = lambda i: (i, 0)
    return pl.pallas_call(
        functools.partial(_conv_qkv_prompt_kernel, tm=tm),
        grid=(rows // tm,),
        in_specs=[pl.BlockSpec((tm, INNER_C), row),
                  pl.BlockSpec((8, INNER_C), lambda i: (jnp.maximum(i * (tm // 8) - 1, 0), 0)),
                  _full(cw.shape), _full(cb.shape), _full(wq_bf.shape), _full(wk_bf.shape), _full(wv_bf.shape)],
        out_specs=[pl.BlockSpec((tm, INNER_C), row)] * 4,
        out_shape=[jax.ShapeDtypeStruct((rows, INNER_C), F32)] + [jax.ShapeDtypeStruct((rows, INNER_C), BF16)] * 3,
        scratch_shapes=[pltpu.VMEM((8 + tm, INNER_C), F32)],
        compiler_params=_cparams("parallel"),
        name="conv_qkv_prompt",
    )(xm, xm, cw, cb, wq_bf, wk_bf, wv_bf)


def _conv_qkv_sample_kernel(ext_ref, cw_ref, cb_ref, wq_ref, wk_ref, wv_ref, ca_ref, q_ref, k_ref, v_ref):
    t = pl.program_id(0)
    bs = ext_ref.shape[1]
    conv = jnp.broadcast_to(cb_ref[...], (bs, INNER_C))
    for j in range(CONV_W):
        conv = conv + ext_ref[t + j] * cw_ref[j:j + 1, :]
    ca = _silu(conv)
    ca_ref[...] = ca
    _qkv_math(ca, ext_ref[t + CONV_W - 1], wq_ref, wk_ref, wv_ref, q_ref, k_ref, v_ref)


def _conv_qkv_sample(ext_t, cw, cb, wq_bf, wk_bf, wv_bf):
    ts = ext_t.shape[0] - (CONV_W - 1)
    bs = ext_t.shape[1]
    row = lambda t: (t, 0)
    return pl.pallas_call(
        _conv_qkv_sample_kernel,
        grid=(ts,),
        in_specs=[_full(ext_t.shape), _full(cw.shape), _full(cb.shape), _full(wq_bf.shape), _full(wk_bf.shape),
                  _full(wv_bf.shape)],
        out_specs=[pl.BlockSpec((bs, INNER_C), row)] * 4,
        out_shape=[jax.ShapeDtypeStruct((ts * bs, INNER_C), F32)]
                  + [jax.ShapeDtypeStruct((ts * bs, INNER_C), BF16)] * 3,
        compiler_params=_cparams("arbitrary"),
        name="conv_qkv_sample",
    )(ext_t, cw, cb, wq_bf, wk_bf, wv_bf)


def _mlstm_heads(qs, ks, vs, g, valid, Cs, ns, ms, lower, sel):
    nh = len(qs)
    H = range(nh)
    T = qs[0].shape[0]
    lane = lax.broadcasted_iota(jnp.int32, g.shape, 1)
    g = jnp.where(valid, g, jnp.where(lane < nh, NEG, 0.0))
    qks = [_dot_nt(qs[h], ks[h]) for h in H]
    qcs = [_dot(qs[h], Cs[h].astype(BF16)) for h in H]
    g_parts = _split3(g)
    cum = sum(_dot(lower, p) for p in g_parts)
    rows_g = sum(_dot_nt(sel, p) for p in g_parts)
    rows_c = sum(_dot_nt(sel, p) for p in _split3(cum))
    tri = lax.broadcasted_iota(jnp.int32, (T, T), 1) <= lax.broadcasted_iota(jnp.int32, (T, T), 0)
    b_cs = [cum[:, nh + h:nh + h + 1] for h in H]
    ds = [jnp.where(tri, b_cs[h] - rows_c[nh + h:nh + h + 1, :] + rows_g[h:h + 1, :], NEG) for h in H]
    m_inters = [b_cs[h] + ms[h] for h in H]
    m_ts = [jnp.maximum(m_inters[h], jnp.max(ds[h], axis=-1, keepdims=True)) for h in H]
    ss = [jnp.where(tri, qks[h] * jnp.exp(ds[h] - m_ts[h]), 0.0) for h in H]
    svs = [_dot(ss[h].astype(BF16), vs[h]) for h in H]
    m_news = [m_ts[h][T - 1:T, :] for h in H]
    b_lasts = [b_cs[h][T - 1:T, :] for h in H]
    kws = [ks[h].astype(F32) * jnp.exp(b_lasts[h] - b_cs[h] + g[:, h:h + 1] - m_news[h]) for h in H]
    upds = [_dot_tn(kws[h].astype(BF16), vs[h]) for h in H]
    decays = [jnp.exp(b_lasts[h] + ms[h] - m_news[h]) for h in H]
    C_news = [decays[h] * Cs[h] + upds[h] for h in H]
    n_news = [decays[h] * ns[h] + jnp.sum(kws[h], axis=0, keepdims=True) for h in H]
    hns = []
    for h in H:
        w_inter = jnp.exp(m_inters[h] - m_ts[h])
        num = w_inter * qcs[h] + svs[h]
        den = (w_inter * jnp.sum(qs[h].astype(F32) * ns[h], axis=-1, keepdims=True)
               + jnp.sum(ss[h], axis=-1, keepdims=True))
        hh = num / jnp.maximum(jnp.abs(den), jnp.exp(-m_ts[h]))
        hc = hh - jnp.mean(hh, axis=-1, keepdims=True)
        var = jnp.mean(hc * hc, axis=-1, keepdims=True)
        hns.append(hc * lax.rsqrt(var + EPS))
    return hns, C_news, n_news, m_news


def _mlstm_step(q_ref, k_ref, v_ref, g_ref, ca_ref, z_ref, og_ref, sk_ref, valid, c_in, n_in, m_in,
                lower_ref, sel_ref, mix_ref, c_ref, n_ref, m_ref):
    H = range(H_C)
    cols = [slice(h * DH_C, (h + 1) * DH_C) for h in H]
    hns, Cs, ns, ms = _mlstm_heads(
        [q_ref[:, c] for c in cols], [k_ref[:, c] for c in cols], [v_ref[:, c] for c in cols], g_ref[...], valid,
        [c_in[h] for h in H], [n_in[h] for h in H], [m_in[h][:, 0:1] for h in H], lower_ref[...], sel_ref[...])
    for h in H:
        c = cols[h]
        mix = (hns[h] * og_ref[:, c] + sk_ref[:, c] * ca_ref[:, c]) * _silu(z_ref[:, c])
        mix_ref[:, c] = mix.astype(mix_ref.dtype)
        c_ref[h] = Cs[h]
        n_ref[h] = ns[h]
        m_ref[h] = jnp.broadcast_to(ms[h], (1, TILE))


def _mlstm_prompt_kernel(q_ref, k_ref, v_ref, g_ref, ca_ref, z_ref, og_ref, sk_ref, lower_ref, sel_ref,
                         mix_ref, c_ref, n_ref, m_ref):
    c = pl.program_id(1)

    @pl.when(c == 0)
    def _():
        c_ref[...] = jnp.zeros_like(c_ref)
        n_ref[...] = jnp.zeros_like(n_ref)
        m_ref[...] = jnp.zeros_like(m_ref)

    valid = c * TILE + lax.broadcasted_iota(jnp.int32, (TILE, 1), 0) >= PAD
    _mlstm_step(q_ref, k_ref, v_ref, g_ref, ca_ref, z_ref, og_ref, sk_ref, valid, c_ref, n_ref, m_ref,
                lower_ref, sel_ref, mix_ref, c_ref, n_ref, m_ref)


def _mlstm_prompt(q, k, v, g, ca, z, og, sk, lower, sel, nb, lp):
    nc = lp // TILE
    blk = lambda b, c: (b * nc + c, 0)
    st = lambda b, c: (b, 0, 0)
    return pl.pallas_call(
        _mlstm_prompt_kernel,
        grid=(nb, nc),
        in_specs=[pl.BlockSpec((TILE, INNER_C), blk)] * 3
                 + [pl.BlockSpec((TILE, TILE), blk), pl.BlockSpec((TILE, INNER_C), blk),
                    pl.BlockSpec((TILE, INNER_C), blk), _full(og.shape), _full(sk.shape),
                    _full(lower.shape), _full(sel.shape)],
        out_specs=[pl.BlockSpec((TILE, INNER_C), blk), pl.BlockSpec((H_C, DH_C, DH_C), st),
                   pl.BlockSpec((H_C, 1, DH_C), st), pl.BlockSpec((H_C, 1, TILE), st)],
        out_shape=[jax.ShapeDtypeStruct((nb * lp, INNER_C), BF16),
                   jax.ShapeDtypeStruct((nb * H_C, DH_C, DH_C), F32),
                   jax.ShapeDtypeStruct((nb * H_C, 1, DH_C), F32),
                   jax.ShapeDtypeStruct((nb * H_C, 1, TILE), F32)],
        compiler_params=_cparams("parallel", "arbitrary"),
        name="mlstm_prompt",
    )(q, k, v, g, ca, z, og, sk, lower, sel)


def _mlstm_sample_kernel(q_ref, k_ref, v_ref, g_ref, ca_ref, z_ref, og_ref, sk_ref, c_in, n_in, m_in,
                         lower_ref, sel_ref, mix_ref, c_ref, n_ref, m_ref, *, ts):
    valid = lax.broadcasted_iota(jnp.int32, (q_ref.shape[0], 1), 0) < ts
    _mlstm_step(q_ref, k_ref, v_ref, g_ref, ca_ref, z_ref, og_ref, sk_ref, valid, c_in, n_in, m_in,
                lower_ref, sel_ref, mix_ref, c_ref, n_ref, m_ref)


def _mlstm_sample(q, k, v, g, ca, z, og, sk, c0, n0, m0, lower, sel, ts):
    bs, rp = q.shape[0], q.shape[1]
    blk = lambda s: (s, 0, 0)
    rows = pl.BlockSpec((None, rp, INNER_C), blk)
    state_specs = [pl.BlockSpec((H_C, DH_C, DH_C), blk), pl.BlockSpec((H_C, 1, DH_C), blk),
                   pl.BlockSpec((H_C, 1, TILE), blk)]
    return pl.pallas_call(
        functools.partial(_mlstm_sample_kernel, ts=ts),
        grid=(bs,),
        in_specs=[rows] * 3 + [pl.BlockSpec((None, rp, TILE), blk), rows, rows,
                               _full(og.shape), _full(sk.shape)]
                 + state_specs + [_full(lower.shape), _full(sel.shape)],
        out_specs=[rows] + state_specs,
        out_shape=[jax.ShapeDtypeStruct((bs, rp, INNER_C), BF16),
                   jax.ShapeDtypeStruct((bs * H_C, DH_C, DH_C), F32),
                   jax.ShapeDtypeStruct((bs * H_C, 1, DH_C), F32),
                   jax.ShapeDtypeStruct((bs * H_C, 1, TILE), F32)],
        compiler_params=_cparams("parallel"),
        name="mlstm_sample",
    )(q, k, v, g, ca, z, og, sk, c0, n0, m0, lower, sel)


def _tail1_kernel(mix_ref, h_ref, wo_ref, fg_ref, y_ref):
    h2 = h_ref[...] + _dot(mix_ref[...], wo_ref[...])
    ms = jnp.mean(h2 * h2, axis=-1, keepdims=True)
    y_ref[...] = h2 * lax.rsqrt(ms + EPS) * fg_ref[...]


def _tail1(mix, h, wo_bf, fg, grid, in_map, out_map, out_rows, tm):
    return pl.pallas_call(
        _tail1_kernel,
        grid=grid,
        in_specs=[pl.BlockSpec((tm, INNER_C), in_map), pl.BlockSpec((tm, D_MODEL), in_map),
                  _full(wo_bf.shape), _full(fg.shape)],
        out_specs=pl.BlockSpec((tm, D_MODEL), out_map),
        out_shape=jax.ShapeDtypeStruct((out_rows, D_MODEL), F32),
        compiler_params=_cparams(*(["parallel"] * len(grid))),
        name="tail1",
    )(mix, h, wo_bf, fg)


def kernel(x_prompt, x_sample, cache_sb_k, cache_sb_v, page_table, state_pool, state_C, state_n, state_m, state_conv, meta_tokens, norm_g, final_norm_g, w_in_a, w_out_a, sb_bias, w_pool, pool_scale, w_in_c, b_gate_c, conv_w, conv_b, w_q, w_k, w_v, skip_c, outnorm_g, w_out_c):
    nb, seq, _ = x_prompt.shape
    bs, ts, _ = x_sample.shape
    lp = PAD + N_META + seq
    n_pool, page = cache_sb_k.shape[1], cache_sb_k.shape[2]
    tm = 2 * TILE

    r = jnp.arange(TILE)
    uinc1 = (r[:, None] >= r[None, :]).astype(BF16)
    uinc = jnp.concatenate([uinc1, uinc1], axis=0)
    lower = (r[None, :] <= r[:, None]).astype(BF16)
    sel = (jnp.arange(16)[:, None] == r[None, :]).astype(BF16)

    w_in_a_bf = w_in_a[0].astype(BF16)
    w_out_a_bf = w_out_a[0].astype(BF16)
    w_pool_bf = w_pool[0].astype(BF16)
    ps = pool_scale[0].reshape(1, W_B)
    g0 = norm_g[0].reshape(1, D_MODEL)
    q_scale = DH_A ** -0.5 * LOG2E
    bias2 = sb_bias[0] * LOG2E
    outs0 = [(0, W_A, q_scale, BF16),
             (W_A, W_A, 1.0, F32), (2 * W_A, W_A, 1.0, F32),
             (W_A, W_A, 1.0, BF16), (2 * W_A, W_A, 1.0, BF16),
             (3 * W_A, W_A, 1.0, F32),
             (4 * W_A, W_B, 1.0, F32), (4 * W_A + W_B, W_B, 1.0, F32)]

    meta = jnp.broadcast_to(meta_tokens.astype(x_prompt.dtype)[None], (nb, N_META, D_MODEL))
    h_p = jnp.concatenate([jnp.zeros((nb, PAD, D_MODEL), x_prompt.dtype), meta, x_prompt], axis=1)
    h_p = h_p.reshape(nb * lp, D_MODEL)
    q_bf, k_p, v_p, k_bf, v_bf, ga_p, u_p, gb_p = _norm_proj(h_p, g0, w_in_a_bf, outs0, tm)
    oa_p = _sb_prompt(bias2, q_bf, k_bf, v_bf, uinc, nb, lp)
    h1_p = _tail0_prompt(oa_p, ga_p, u_p, gb_p, h_p, w_pool_bf, ps, w_out_a_bf, tm, lp)

    def to_tm(a):
        return a.transpose(1, 0, 2).reshape(ts * bs, a.shape[-1])

    def to_sm(a):
        return a.reshape(ts, bs, a.shape[-1]).transpose(1, 0, 2)

    h_s = to_tm(x_sample)
    outs0_s = [(0, W_A, q_scale, F32)] + outs0[1:3] + outs0[5:]
    q_s, k_s, v_s, ga_s, u_s, gb_s = _norm_proj(h_s, g0, w_in_a_bf, outs0_s, tm)
    q_rep = jnp.repeat(to_sm(q_s), H_A, axis=1)
    row_head = jnp.arange(ts * H_A) % H_A
    hm = (row_head[:, None] == (jnp.arange(W_A) // DH_A)[None, :]).astype(F32)
    bias_rows = jnp.broadcast_to(bias2[row_head][:, None], (ts * H_A, TILE))

    def pages_t(c):
        return c.transpose(0, 2, 3, 1).reshape(n_pool, W_A, page)

    oa_s = _sb_sample(page_table, q_rep, to_sm(k_s), to_sm(v_s), hm, bias_rows, uinc,
                      pages_t(cache_sb_k[0]), pages_t(cache_sb_v[0]))
    pool_ext = jnp.concatenate([state_pool[0].transpose(1, 0, 2), u_s.reshape(ts, bs, W_B)], axis=0)
    h1_s = _tail0_sample(to_tm(oa_s), ga_s, pool_ext, gb_s, h_s, w_pool_bf, ps, w_out_a_bf)

    g1 = norm_g[1].reshape(1, D_MODEL)
    w_in_c_bf = w_in_c[0][:, :2 * INNER_C].astype(BF16)
    n_gate = 2 * H_C
    wg_bf = jnp.pad(w_in_c[0][:, 2 * INNER_C:], ((0, 0), (0, TILE - n_gate))).astype(BF16)
    bg = jnp.pad(b_gate_c[0], (0, TILE - n_gate)).reshape(1, TILE)
    cw, cb = conv_w[0], conv_b[0].reshape(1, INNER_C)
    wq_bf, wk_bf, wv_bf = w_q[0].astype(BF16), w_k[0].astype(BF16), w_v[0].astype(BF16)
    og, sk = outnorm_g[0].reshape(1, INNER_C), skip_c[0].reshape(1, INNER_C)
    w_out_c_bf = w_out_c[0].astype(BF16)
    fg = final_norm_g.reshape(1, D_MODEL)

    xm_p, z_p, gate_p = _norm_proj1(h1_p, g1, w_in_c_bf, wg_bf, bg, tm)
    ca_p, q1_p, k1_p, v1_p = _conv_qkv_prompt(xm_p, cw, cb, wq_bf, wk_bf, wv_bf, tm)
    mix_p, C_p, n_p, m_p = _mlstm_prompt(q1_p, k1_p, v1_p, gate_p, ca_p, z_p, og, sk, lower, sel, nb, lp)
    nc = lp // TILE
    y_prompt = _tail1(mix_p, h1_p, w_out_c_bf, fg, (nb, nc - 1),
                      lambda b, i: (b * nc + 1 + i, 0), lambda b, i: (b * (nc - 1) + i, 0), nb * seq, TILE)

    xm_s, z_s, gate_s = _norm_proj1(h1_s, g1, w_in_c_bf, wg_bf, bg, tm)
    conv_ext = jnp.concatenate([state_conv[0].transpose(1, 0, 2), xm_s.reshape(ts, bs, INNER_C)], axis=0)
    ca_s, q1_s, k1_s, v1_s = _conv_qkv_sample(conv_ext, cw, cb, wq_bf, wk_bf, wv_bf)
    rp = 16

    def to_rows(a):
        return jnp.pad(to_sm(a), ((0, 0), (0, rp - ts), (0, 0)))

    m0 = jnp.broadcast_to(state_m[0].reshape(bs * H_C, 1, 1), (bs * H_C, 1, TILE))
    mix_s, C_s, n_s, m_s = _mlstm_sample(to_rows(q1_s), to_rows(k1_s), to_rows(v1_s), to_rows(gate_s),
                                         to_rows(ca_s), to_rows(z_s), og, sk,
                                         state_C[0].reshape(bs * H_C, DH_C, DH_C),
                                         state_n[0].reshape(bs * H_C, 1, DH_C), m0, lower[:rp, :rp], sel, ts)
    y_s = _tail1(to_tm(mix_s[:, :ts]), h1_s, w_out_c_bf, fg, (ts * bs // tm,),
                 lambda i: (i, 0), lambda i: (i, 0), ts * bs, tm)

    def heads(a):
        return a.reshape(a.shape[:-1] + (H_A, DH_A))

    k3, v3, u3 = (a.reshape(nb, lp, -1) for a in (k_p, v_p, u_p))
    return (y_prompt.reshape(nb, seq, D_MODEL), to_sm(y_s),
            heads(k3[:, PAD:])[None], heads(v3[:, PAD:])[None],
            heads(to_sm(k_s))[None], heads(to_sm(v_s))[None],
            u3[:, lp - (POOL_MAX - 1):][None], pool_ext[ts:].transpose(1, 0, 2)[None],
            C_p.reshape(1, nb, H_C, DH_C, DH_C), C_s.reshape(1, bs, H_C, DH_C, DH_C),
            n_p.reshape(1, nb, H_C, DH_C), n_s.reshape(1, bs, H_C, DH_C),
            m_p[:, 0, 0].reshape(1, nb, H_C), m_s[:, 0, 0].reshape(1, bs, H_C),
            xm_p.reshape(nb, lp, INNER_C)[:, lp - (CONV_W - 1):][None], conv_ext[ts:].transpose(1, 0, 2)[None])
```

```python
import functools

import jax
import jax.numpy as jnp
from jax import lax
from jax.experimental import pallas as pl
from jax.experimental.pallas import tpu as pltpu

F32 = jnp.float32
BF16 = jnp.bfloat16

D_MODEL = 1024
N_META = 16
W_A = 512
H_A = 8
DH_A = 64
W_B = 512
POOL_WINDOWS = (2, 4, 8, 16)
POOL_GC = 128
POOL_MAX = 16
INNER_C = 2048
H_C = 4
DH_C = 512
CONV_W = 4
EPS = 1e-6

TILE = 128
PAD = TILE - N_META
NEG = -1e30
VMEM_LIMIT = 56 * 1024 * 1024


def _cparams(*sem):
    return pltpu.CompilerParams(dimension_semantics=sem, vmem_limit_bytes=VMEM_LIMIT)


def _silu(x):
    return x / (1.0 + jnp.exp(-x))


def _softplus(z):
    return jnp.maximum(z, 0.0) + jnp.log(1.0 + jnp.exp(-jnp.abs(z)))


def _split2(x):
    hi = x.astype(BF16)
    lo = (x - hi.astype(F32)).astype(BF16)
    return hi, lo


def _split3(x):
    hi = x.astype(BF16)
    r = x - hi.astype(F32)
    mid = r.astype(BF16)
    lo = (r - mid.astype(F32)).astype(BF16)
    return hi, mid, lo


def _dot(a, b):
    return jnp.dot(a, b, preferred_element_type=F32)


def _dot_nt(a, b):
    return lax.dot_general(a, b, (((1,), (1,)), ((), ())), preferred_element_type=F32)


def _dot_tn(a, b):
    return lax.dot_general(a, b, (((0,), (0,)), ((), ())), preferred_element_type=F32)


def _full(shape):
    nd = len(shape)
    return pl.BlockSpec(shape, lambda *_: (0,) * nd)


def _norm_proj_kernel(x_ref, g_ref, w_ref, *out_refs, outs):
    x = x_ref[...]
    ms = jnp.mean(x * x, axis=-1, keepdims=True)
    xn = (x * lax.rsqrt(ms + EPS) * g_ref[...]).astype(BF16)
    accs = {}
    for o_ref, (off, width, scale) in zip(out_refs, outs):
        if (off, width) not in accs:
            accs[(off, width)] = _dot(xn, w_ref[:, off:off + width])
        acc = accs[(off, width)]
        if scale != 1.0:
            acc = acc * scale
        o_ref[...] = acc.astype(o_ref.dtype)


def _norm_proj(x, g, w_bf, outs, tm):
    rows = x.shape[0]
    kern = functools.partial(_norm_proj_kernel, outs=[(o, w, s) for o, w, s, _ in outs])
    return pl.pallas_call(
        kern,
        grid=(rows // tm,),
        in_specs=[pl.BlockSpec((tm, D_MODEL), lambda i: (i, 0)), _full((1, D_MODEL)), _full(w_bf.shape)],
        out_specs=[pl.BlockSpec((tm, w), lambda i: (i, 0)) for _, w, _, _ in outs],
        out_shape=[jax.ShapeDtypeStruct((rows, w), dt) for _, w, _, dt in outs],
        compiler_params=_cparams("parallel"),
        name="norm_proj",
    )(x, g, w_bf)


def _norm_proj1_kernel(x_ref, g_ref, w_ref, wg_ref, bg_ref, xm_ref, z_ref, gate_ref):
    x = x_ref[...]
    ms = jnp.mean(x * x, axis=-1, keepdims=True)
    xn = (x * lax.rsqrt(ms + EPS) * g_ref[...]).astype(BF16)
    xm_ref[...] = _dot(xn, w_ref[:, :INNER_C])
    z_ref[...] = _dot(xn, w_ref[:, INNER_C:])
    gl = _dot(xn, wg_ref[...]) + bg_ref[...]
    lane = lax.broadcasted_iota(jnp.int32, gl.shape, 1)
    gate_ref[...] = jnp.where(lane >= H_C, -_softplus(-gl), gl)


def _norm_proj1(x, g, w_bf, wg_bf, bg, tm):
    rows = x.shape[0]
    return pl.pallas_call(
        _norm_proj1_kernel,
        grid=(rows // tm,),
        in_specs=[pl.BlockSpec((tm, D_MODEL), lambda i: (i, 0)), _full((1, D_MODEL)), _full(w_bf.shape),
                  _full(wg_bf.shape), _full(bg.shape)],
        out_specs=[pl.BlockSpec((tm, INNER_C), lambda i: (i, 0)), pl.BlockSpec((tm, INNER_C), lambda i: (i, 0)),
                   pl.BlockSpec((tm, TILE), lambda i: (i, 0))],
        out_shape=[jax.ShapeDtypeStruct((rows, INNER_C), F32), jax.ShapeDtypeStruct((rows, INNER_C), F32),
                   jax.ShapeDtypeStruct((rows, TILE), F32)],
        compiler_params=_cparams("parallel"),
        name="norm_proj1",
    )(x, g, w_bf, wg_bf, bg)


LOG2E = 1.4426950408889634


def _softplus2(y):
    return jnp.where(y > 64.0, y, jnp.log2(1.0 + jnp.exp2(y)))


def _sb_prompt_kernel(bias_ref, q_ref, k_ref, v_ref, uinc_ref, o_ref, qm_ref):
    qi = pl.program_id(1)
    uinc = uinc_ref[...]
    lane = lax.broadcasted_iota(jnp.int32, (TILE, TILE), 1)
    first_half = lane < DH_A
    causal = lane < lax.broadcasted_iota(jnp.int32, (TILE, TILE), 0)
    heads = range(H_A)
    pairs = range(H_A // 2)
    for p in pairs:
        qp = q_ref[:, p * TILE:(p + 1) * TILE]
        qm_ref[2 * p] = jnp.where(first_half, qp, jnp.zeros_like(qp))
        qm_ref[2 * p + 1] = jnp.where(first_half, jnp.zeros_like(qp), qp)

    def key_tiles(kss, masks, accs, cbs):
        tiles = range(len(kss))
        kps = [[k_ref[pl.ds(ks, TILE), p * TILE:(p + 1) * TILE] for p in pairs] for ks in kss]
        vps = [[v_ref[pl.ds(ks, TILE), p * TILE:(p + 1) * TILE] for p in pairs] for ks in kss]
        qps = [qm_ref[2 * p:2 * p + 2].reshape(2 * TILE, TILE) for p in pairs]
        zps = [[_dot_nt(qps[p], kps[i][p]) for p in pairs] for i in tiles]
        zs, ts = [], []
        group = H_A // 2
        for i in tiles:
            zs.append([zps[i][h // 2][(h % 2) * TILE:(h % 2 + 1) * TILE] + bias_ref[h] for h in heads])
            ts.append([])
            for g0 in range(0, H_A, group):
                drops = [_softplus2(z) for z in zs[i][g0:g0 + group]]
                if masks[i] is not None:
                    drops = [jnp.where(masks[i], d, 0.0) for d in drops]
                tall = _dot(jnp.concatenate([d.astype(BF16) for d in drops], axis=0), uinc)
                ts[i] += [tall[h * TILE:(h + 1) * TILE] for h in range(group)]
        cbs = list(cbs)
        accs = list(accs)
        for i in tiles:
            ws = [jnp.exp2(zs[i][h] - ts[i][h] - cbs[h]) for h in heads]
            if masks[i] is not None:
                ws = [jnp.where(masks[i], a, 0.0) for a in ws]
            cbs = [cbs[h] + jnp.broadcast_to(ts[i][h][:, 0:1], (TILE, TILE)) for h in heads]
            for p in pairs:
                r = _dot(jnp.concatenate([ws[2 * p].astype(BF16), ws[2 * p + 1].astype(BF16)], axis=0), vps[i][p])
                accs[p] = accs[p] + jnp.where(first_half, r[:TILE], r[TILE:])
        return tuple(accs), tuple(cbs)

    zero = jnp.zeros((TILE, TILE), F32)
    carry = ((zero,) * (H_A // 2), (zero,) * H_A)

    def pair(ks, first_mask, carry):
        ks = pl.multiple_of(ks, TILE)
        return key_tiles([ks, pl.multiple_of(ks - TILE, TILE)], [first_mask, None], *carry)

    carry = lax.cond(qi == 0, lambda c: key_tiles([0], [causal], *c), lambda c: pair(qi * TILE, causal, c), carry)
    carry = lax.fori_loop(0, jnp.maximum(qi - 1, 0) // 2, lambda j, c: pair((qi - 2 - 2 * j) * TILE, None, c), carry)
    accs, _ = lax.cond(jnp.logical_and(qi >= 2, qi % 2 == 0), lambda c: key_tiles([0], [None], *c), lambda c: c,
                       carry)
    for p in pairs:
        o_ref[:, p * TILE:(p + 1) * TILE] = accs[p]


def _sb_prompt(bias, q_bf, k_bf, v_bf, uinc, nb, lp):
    nq = lp // TILE
    return pl.pallas_call(
        _sb_prompt_kernel,
        grid=(nb, nq),
        in_specs=[pl.BlockSpec(memory_space=pltpu.SMEM),
                  pl.BlockSpec((TILE, W_A), lambda b, i: (b * nq + i, 0)),
                  pl.BlockSpec((lp, W_A), lambda b, i: (b, 0)),
                  pl.BlockSpec((lp, W_A), lambda b, i: (b, 0)),
                  _full(uinc.shape)],
        out_specs=pl.BlockSpec((TILE, W_A), lambda b, i: (b * nq + i, 0)),
        out_shape=jax.ShapeDtypeStruct((nb * lp, W_A), F32),
        scratch_shapes=[pltpu.VMEM((H_A, TILE, TILE), BF16)],
        compiler_params=_cparams("parallel", "arbitrary"),
        name="sb_prompt",
    )(bias, q_bf, k_bf, v_bf, uinc)


def _sb_sample_kernel(pt_ref, q_ref, kn_ref, vn_ref, hm_ref, bias_ref, uinc_ref, *rest, n_pages, ts):
    kp_refs = rest[:n_pages]
    vp_refs = rest[n_pages:2 * n_pages]
    o_ref = rest[2 * n_pages]
    hm = hm_ref[...]
    qbd = q_ref[...] * hm
    bias = bias_ref[...]
    uinc = uinc_ref[...]
    rows = ts * H_A
    t_of_row = lax.broadcasted_iota(jnp.int32, (rows, 1), 0) // H_A
    out = jnp.zeros((rows, W_A), F32)
    carry = jnp.zeros((rows, 1), F32)
    for j in range(ts - 1, -1, -1):
        z = jnp.sum(qbd * kn_ref[j:j + 1, :], axis=-1, keepdims=True) + bias[:, 0:1]
        sp = _softplus2(z)
        m = j < t_of_row
        a = jnp.where(m, jnp.exp2(z - sp - carry), 0.0)
        carry = carry + jnp.where(m, sp, 0.0)
        out = out + a * vn_ref[j:j + 1, :]
    cb = jnp.broadcast_to(carry, (rows, TILE))
    qbd_bf = qbd.astype(BF16)
    pages = range(n_pages - 1, -1, -1)
    zs = [_dot(qbd_bf, kp_refs[p][...].astype(BF16)) + bias for p in pages]
    sps = [_softplus2(z) for z in zs]
    tall = _dot(jnp.concatenate([jnp.concatenate(_split2(sp), axis=1) for sp in sps], axis=0), uinc)
    for i, p in enumerate(pages):
        tail = tall[i * rows:(i + 1) * rows]
        a = jnp.exp2(zs[i] - tail - cb)
        cb = cb + jnp.broadcast_to(tail[:, 0:1], (rows, TILE))
        out = out + _dot_nt(a.astype(BF16), vp_refs[p][...].astype(BF16))
    o_ref[...] = jnp.sum((out * hm).reshape(ts, H_A, W_A), axis=1)


def _sb_sample(page_table, q_rep, kn, vn, hm, bias_rows, uinc, cache_k, cache_v):
    bs, n_pages = page_table.shape
    ts = kn.shape[1]
    page = cache_k.shape[2]
    rows = ts * H_A

    def page_spec(p):
        return pl.BlockSpec((None, W_A, page), lambda s, pt: (pt[s * n_pages + p], 0, 0))

    grid_spec = pltpu.PrefetchScalarGridSpec(
        num_scalar_prefetch=1,
        grid=(bs,),
        in_specs=[pl.BlockSpec((None, rows, W_A), lambda s, pt: (s, 0, 0)),
                  pl.BlockSpec((None, ts, W_A), lambda s, pt: (s, 0, 0)),
                  pl.BlockSpec((None, ts, W_A), lambda s, pt: (s, 0, 0)),
                  pl.BlockSpec(hm.shape, lambda s, pt: (0, 0)),
                  pl.BlockSpec(bias_rows.shape, lambda s, pt: (0, 0)),
                  pl.BlockSpec(uinc.shape, lambda s, pt: (0, 0))]
                 + [page_spec(p) for p in range(n_pages)] * 2,
        out_specs=pl.BlockSpec((None, ts, W_A), lambda s, pt: (s, 0, 0)),
    )
    return pl.pallas_call(
        functools.partial(_sb_sample_kernel, n_pages=n_pages, ts=ts),
        grid_spec=grid_spec,
        out_shape=jax.ShapeDtypeStruct((bs, ts, W_A), F32),
        compiler_params=_cparams("arbitrary"),
        name="sb_sample",
    )(page_table.reshape(-1), q_rep, kn, vn, hm, bias_rows, uinc,
      *([cache_k] * n_pages), *([cache_v] * n_pages))


def _tail0_math(o_a, g_a, u, win_sums, cnts, g_b, h, wp_ref, ps, wo_ref):
    parts = [(o_a * _silu(g_a)).astype(BF16)]
    for g in range(len(POOL_WINDOWS)):
        cs = slice(g * POOL_GC, (g + 1) * POOL_GC)
        diff = win_sums[g] / cnts[g] - u[:, cs]
        y = _dot(diff.astype(BF16), wp_ref[g]) * ps[:, cs]
        parts.append((y * _silu(g_b[:, cs])).astype(BF16))
    mixed = jnp.concatenate(parts, axis=1)
    return h + _dot(mixed, wo_ref[...])


def _tail0_prompt_kernel(oa_ref, ga_ref, u_ref, halo_ref, gb_ref, h_ref, wp_ref, ps_ref, wo_ref, out_ref,
                         ext_ref, *, tm, lp):
    i = pl.program_id(0)
    ext_ref[0:POOL_MAX, :] = halo_ref[...]
    ext_ref[POOL_MAX:POOL_MAX + tm, :] = u_ref[...]
    row = (i * tm) % lp + lax.broadcasted_iota(jnp.int32, (tm, 1), 0)
    pos = jnp.where(row >= lp, row - lp, row) - PAD
    win_sums, cnts = [], []
    for g, w in enumerate(POOL_WINDOWS):
        cs = slice(g * POOL_GC, (g + 1) * POOL_GC)
        s = ext_ref[POOL_MAX:POOL_MAX + tm, cs]
        for j in range(1, w):
            s = s + ext_ref[POOL_MAX - j:POOL_MAX - j + tm, cs]
        win_sums.append(s)
        cnts.append(jnp.clip(pos + 1, 1, w).astype(F32))
    out_ref[...] = _tail0_math(oa_ref[...], ga_ref[...], u_ref[...], win_sums, cnts, gb_ref[...], h_ref[...],
                               wp_ref, ps_ref[...], wo_ref)


def _tail0_prompt(o_a, g_a, u, g_b, h, wp_bf, ps, wo_bf, tm, lp):
    rows = h.shape[0]
    hb = tm // POOL_MAX
    row = lambda i: (i, 0)
    return pl.pallas_call(
        functools.partial(_tail0_prompt_kernel, tm=tm, lp=lp),
        grid=(rows // tm,),
        in_specs=[pl.BlockSpec((tm, W_A), row), pl.BlockSpec((tm, W_A), row), pl.BlockSpec((tm, W_B), row),
                  pl.BlockSpec((POOL_MAX, W_B), lambda i: (jnp.maximum(i * hb - 1, 0), 0)),
                  pl.BlockSpec((tm, W_B), row), pl.BlockSpec((tm, D_MODEL), row),
                  _full(wp_bf.shape), _full(ps.shape), _full(wo_bf.shape)],
        out_specs=pl.BlockSpec((tm, D_MODEL), row),
        out_shape=jax.ShapeDtypeStruct((rows, D_MODEL), F32),
        scratch_shapes=[pltpu.VMEM((POOL_MAX + tm, W_B), F32)],
        compiler_params=_cparams("parallel"),
        name="tail0_prompt",
    )(o_a, g_a, u, u, g_b, h, wp_bf, ps, wo_bf)


def _tail0_sample_kernel(oa_ref, ga_ref, ext_ref, gb_ref, h_ref, wp_ref, ps_ref, wo_ref, out_ref, *, ts, bs):
    pre = POOL_MAX - 1
    u = jnp.concatenate([ext_ref[pre + t] for t in range(ts)], axis=0)
    win_sums, cnts = [], []
    for g, w in enumerate(POOL_WINDOWS):
        cs = slice(g * POOL_GC, (g + 1) * POOL_GC)
        per_t = []
        for t in range(ts):
            s = ext_ref[pre + t, :, cs]
            for j in range(1, w):
                s = s + ext_ref[pre + t - j, :, cs]
            per_t.append(s)
        win_sums.append(jnp.concatenate(per_t, axis=0))
        cnts.append(float(w))
    out_ref[...] = _tail0_math(oa_ref[...], ga_ref[...], u, win_sums, cnts, gb_ref[...], h_ref[...],
                               wp_ref, ps_ref[...], wo_ref)


def _tail0_sample(o_a, g_a, ext_t, g_b, h, wp_bf, ps, wo_bf):
    rows = h.shape[0]
    ts = ext_t.shape[0] - (POOL_MAX - 1)
    args = (o_a, g_a, ext_t, g_b, h, wp_bf, ps, wo_bf)
    return pl.pallas_call(
        functools.partial(_tail0_sample_kernel, ts=ts, bs=ext_t.shape[1]),
        grid=(1,),
        in_specs=[_full(a.shape) for a in args],
        out_specs=_full((rows, D_MODEL)),
        out_shape=jax.ShapeDtypeStruct((rows, D_MODEL), F32),
        compiler_params=_cparams("arbitrary"),
        name="tail0_sample",
    )(*args)


def _qkv_math(ca, xm, wq_ref, wk_ref, wv_ref, q_ref, k_ref, v_ref):
    ca_bf = ca.astype(BF16)
    xm_bf = xm.astype(BF16)
    for h in range(H_C):
        cs = slice(h * DH_C, (h + 1) * DH_C)
        q_ref[:, cs] = _dot(ca_bf[:, cs], wq_ref[h]).astype(BF16)
        k_ref[:, cs] = (_dot(ca_bf[:, cs], wk_ref[h]) * (DH_C ** -0.5)).astype(BF16)
        v_ref[:, cs] = _dot(xm_bf[:, cs], wv_ref[h]).astype(BF16)


def _conv_qkv_prompt_kernel(xm_ref, halo_ref, cw_ref, cb_ref, wq_ref, wk_ref, wv_ref,
                            ca_ref, q_ref, k_ref, v_ref, ext_ref, *, tm):
    ext_ref[0:8, :] = halo_ref[...]
    ext_ref[8:8 + tm, :] = xm_ref[...]
    conv = jnp.broadcast_to(cb_ref[...], (tm, INNER_C))
    for j in range(CONV_W):
        s = 8 - (CONV_W - 1) + j
        conv = conv + ext_ref[s:s + tm, :] * cw_ref[j:j + 1, :]
    ca = _silu(conv)
    ca_ref[...] = ca
    _qkv_math(ca, xm_ref[...], wq_ref, wk_ref, wv_ref, q_ref, k_ref, v_ref)


def _conv_qkv_prompt(xm, cw, cb, wq_bf, wk_bf, wv_bf, tm):
    rows = xm.shape[0]
    row = lambda i: (i, 0)
    return pl.pallas_call(
        functools.partial(_conv_qkv_prompt_kernel, tm=tm),
        grid=(rows // tm,),
        in_specs=[pl.BlockSpec((tm, INNER_C), row),
                  pl.BlockSpec((8, INNER_C), lambda i: (jnp.maximum(i * (tm // 8) - 1, 0), 0)),
                  _full(cw.shape), _full(cb.shape), _full(wq_bf.shape), _full(wk_bf.shape), _full(wv_bf.shape)],
        out_specs=[pl.BlockSpec((tm, INNER_C), row)] * 4,
        out_shape=[jax.ShapeDtypeStruct((rows, INNER_C), F32)] + [jax.ShapeDtypeStruct((rows, INNER_C), BF16)] * 3,
        scratch_shapes=[pltpu.VMEM((8 + tm, INNER_C), F32)],
        compiler_params=_cparams("parallel"),
        name="conv_qkv_prompt",
    )(xm, xm, cw, cb, wq_bf, wk_bf, wv_bf)


def _conv_qkv_sample_kernel(ext_ref, cw_ref, cb_ref, wq_ref, wk_ref, wv_ref, ca_ref, q_ref, k_ref, v_ref):
    t = pl.program_id(0)
    bs = ext_ref.shape[1]
    conv = jnp.broadcast_to(cb_ref[...], (bs, INNER_C))
    for j in range(CONV_W):
        conv = conv + ext_ref[t + j] * cw_ref[j:j + 1, :]
    ca = _silu(conv)
    ca_ref[...] = ca
    _qkv_math(ca, ext_ref[t + CONV_W - 1], wq_ref, wk_ref, wv_ref, q_ref, k_ref, v_ref)


def _conv_qkv_sample(ext_t, cw, cb, wq_bf, wk_bf, wv_bf):
    ts = ext_t.shape[0] - (CONV_W - 1)
    bs = ext_t.shape[1]
    row = lambda t: (t, 0)
    return pl.pallas_call(
        _conv_qkv_sample_kernel,
        grid=(ts,),
        in_specs=[_full(ext_t.shape), _full(cw.shape), _full(cb.shape), _full(wq_bf.shape), _full(wk_bf.shape),
                  _full(wv_bf.shape)],
        out_specs=[pl.BlockSpec((bs, INNER_C), row)] * 4,
        out_shape=[jax.ShapeDtypeStruct((ts * bs, INNER_C), F32)]
                  + [jax.ShapeDtypeStruct((ts * bs, INNER_C), BF16)] * 3,
        compiler_params=_cparams("arbitrary"),
        name="conv_qkv_sample",
    )(ext_t, cw, cb, wq_bf, wk_bf, wv_bf)


def _mlstm_heads(qs, ks, vs, g, valid, Cs, ns, ms, lower, sel):
    nh = len(qs)
    H = range(nh)
    T = qs[0].shape[0]
    lane = lax.broadcasted_iota(jnp.int32, g.shape, 1)
    g = jnp.where(valid, g, jnp.where(lane < nh, NEG, 0.0))
    qks = [_dot_nt(qs[h], ks[h]) for h in H]
    qcs = [_dot(qs[h], Cs[h].astype(BF16)) for h in H]
    g_parts = _split3(g)
    cum = sum(_dot(lower, p) for p in g_parts)
    rows_g = sum(_dot_nt(sel, p) for p in g_parts)
    rows_c = sum(_dot_nt(sel, p) for p in _split3(cum))
    tri = lax.broadcasted_iota(jnp.int32, (T, T), 1) <= lax.broadcasted_iota(jnp.int32, (T, T), 0)
    b_cs = [cum[:, nh + h:nh + h + 1] for h in H]
    ds = [jnp.where(tri, b_cs[h] - rows_c[nh + h:nh + h + 1, :] + rows_g[h:h + 1, :], NEG) for h in H]
    m_inters = [b_cs[h] + ms[h] for h in H]
    m_ts = [jnp.maximum(m_inters[h], jnp.max(ds[h], axis=-1, keepdims=True)) for h in H]
    ss = [jnp.where(tri, qks[h] * jnp.exp(ds[h] - m_ts[h]), 0.0) for h in H]
    svs = [_dot(ss[h].astype(BF16), vs[h]) for h in H]
    m_news = [m_ts[h][T - 1:T, :] for h in H]
    b_lasts = [b_cs[h][T - 1:T, :] for h in H]
    kws = [ks[h].astype(F32) * jnp.exp(b_lasts[h] - b_cs[h] + g[:, h:h + 1] - m_news[h]) for h in H]
    upds = [_dot_tn(kws[h].astype(BF16), vs[h]) for h in H]
    decays = [jnp.exp(b_lasts[h] + ms[h] - m_news[h]) for h in H]
    C_news = [decays[h] * Cs[h] + upds[h] for h in H]
    n_news = [decays[h] * ns[h] + jnp.sum(kws[h], axis=0, keepdims=True) for h in H]
    hns = []
    for h in H:
        w_inter = jnp.exp(m_inters[h] - m_ts[h])
        num = w_inter * qcs[h] + svs[h]
        den = (w_inter * jnp.sum(qs[h].astype(F32) * ns[h], axis=-1, keepdims=True)
               + jnp.sum(ss[h], axis=-1, keepdims=True))
        hh = num / jnp.maximum(jnp.abs(den), jnp.exp(-m_ts[h]))
        hc = hh - jnp.mean(hh, axis=-1, keepdims=True)
        var = jnp.mean(hc * hc, axis=-1, keepdims=True)
        hns.append(hc * lax.rsqrt(var + EPS))
    return hns, C_news, n_news, m_news


def _mlstm_step(q_ref, k_ref, v_ref, g_ref, ca_ref, z_ref, og_ref, sk_ref, valid, c_in, n_in, m_in,
                lower_ref, sel_ref, mix_ref, c_ref, n_ref, m_ref):
    H = range(H_C)
    cols = [slice(h * DH_C, (h + 1) * DH_C) for h in H]
    hns, Cs, ns, ms = _mlstm_heads(
        [q_ref[:, c] for c in cols], [k_ref[:, c] for c in cols], [v_ref[:, c] for c in cols], g_ref[...], valid,
        [c_in[h] for h in H], [n_in[h] for h in H], [m_in[h][:, 0:1] for h in H], lower_ref[...], sel_ref[...])
    for h in H:
        c = cols[h]
        mix = (hns[h] * og_ref[:, c] + sk_ref[:, c] * ca_ref[:, c]) * _silu(z_ref[:, c])
        mix_ref[:, c] = mix.astype(mix_ref.dtype)
        c_ref[h] = Cs[h]
        n_ref[h] = ns[h]
        m_ref[h] = jnp.broadcast_to(ms[h], (1, TILE))


def _mlstm_prompt_kernel(q_ref, k_ref, v_ref, g_ref, ca_ref, z_ref, og_ref, sk_ref, lower_ref, sel_ref,
                         mix_ref, c_ref, n_ref, m_ref):
    c = pl.program_id(1)

    @pl.when(c == 0)
    def _():
        c_ref[...] = jnp.zeros_like(c_ref)
        n_ref[...] = jnp.zeros_like(n_ref)
        m_ref[...] = jnp.zeros_like(m_ref)

    valid = c * TILE + lax.broadcasted_iota(jnp.int32, (TILE, 1), 0) >= PAD
    _mlstm_step(q_ref, k_ref, v_ref, g_ref, ca_ref, z_ref, og_ref, sk_ref, valid, c_ref, n_ref, m_ref,
                lower_ref, sel_ref, mix_ref, c_ref, n_ref, m_ref)


def _mlstm_prompt(q, k, v, g, ca, z, og, sk, lower, sel, nb, lp):
    nc = lp // TILE
    blk = lambda b, c: (b * nc + c, 0)
    st = lambda b, c: (b, 0, 0)
    return pl.pallas_call(
        _mlstm_prompt_kernel,
        grid=(nb, nc),
        in_specs=[pl.BlockSpec((TILE, INNER_C), blk)] * 3
                 + [pl.BlockSpec((TILE, TILE), blk), pl.BlockSpec((TILE, INNER_C), blk),
                    pl.BlockSpec((TILE, INNER_C), blk), _full(og.shape), _full(sk.shape),
                    _full(lower.shape), _full(sel.shape)],
        out_specs=[pl.BlockSpec((TILE, INNER_C), blk), pl.BlockSpec((H_C, DH_C, DH_C), st),
                   pl.BlockSpec((H_C, 1, DH_C), st), pl.BlockSpec((H_C, 1, TILE), st)],
        out_shape=[jax.ShapeDtypeStruct((nb * lp, INNER_C), BF16),
                   jax.ShapeDtypeStruct((nb * H_C, DH_C, DH_C), F32),
                   jax.ShapeDtypeStruct((nb * H_C, 1, DH_C), F32),
                   jax.ShapeDtypeStruct((nb * H_C, 1, TILE), F32)],
        compiler_params=_cparams("parallel", "arbitrary"),
        name="mlstm_prompt",
    )(q, k, v, g, ca, z, og, sk, lower, sel)


def _mlstm_sample_kernel(q_ref, k_ref, v_ref, g_ref, ca_ref, z_ref, og_ref, sk_ref, c_in, n_in, m_in,
                         lower_ref, sel_ref, mix_ref, c_ref, n_ref, m_ref, *, ts):
    valid = lax.broadcasted_iota(jnp.int32, (q_ref.shape[0], 1), 0) < ts
    _mlstm_step(q_ref, k_ref, v_ref, g_ref, ca_ref, z_ref, og_ref, sk_ref, valid, c_in, n_in, m_in,
                lower_ref, sel_ref, mix_ref, c_ref, n_ref, m_ref)


def _mlstm_sample(q, k, v, g, ca, z, og, sk, c0, n0, m0, lower, sel, ts):
    bs, rp = q.shape[0], q.shape[1]
    blk = lambda s: (s, 0, 0)
    rows = pl.BlockSpec((None, rp, INNER_C), blk)
    state_specs = [pl.BlockSpec((H_C, DH_C, DH_C), blk), pl.BlockSpec((H_C, 1, DH_C), blk),
                   pl.BlockSpec((H_C, 1, TILE), blk)]
    return pl.pallas_call(
        functools.partial(_mlstm_sample_kernel, ts=ts),
        grid=(bs,),
        in_specs=[rows] * 3 + [pl.BlockSpec((None, rp, TILE), blk), rows, rows,
                               _full(og.shape), _full(sk.shape)]
                 + state_specs + [_full(lower.shape), _full(sel.shape)],
        out_specs=[rows] + state_specs,
        out_shape=[jax.ShapeDtypeStruct((bs, rp, INNER_C), BF16),
                   jax.ShapeDtypeStruct((bs * H_C, DH_C, DH_C), F32),
                   jax.ShapeDtypeStruct((bs * H_C, 1, DH_C), F32),
                   jax.ShapeDtypeStruct((bs * H_C, 1, TILE), F32)],
        compiler_params=_cparams("parallel"),
        name="mlstm_sample",
    )(q, k, v, g, ca, z, og, sk, c0, n0, m0, lower, sel)


def _tail1_kernel(mix_ref, h_ref, wo_ref, fg_ref, y_ref):
    h2 = h_ref[...] + _dot(mix_ref[...], wo_ref[...])
    ms = jnp.mean(h2 * h2, axis=-1, keepdims=True)
    y_ref[...] = h2 * lax.rsqrt(ms + EPS) * fg_ref[...]


def _tail1(mix, h, wo_bf, fg, grid, in_map, out_map, out_rows, tm, element_rows=False):
    dim = pl.Element if element_rows else (lambda n: n)
    return pl.pallas_call(
        _tail1_kernel,
        grid=grid,
        in_specs=[pl.BlockSpec((dim(tm), dim(INNER_C)), in_map), pl.BlockSpec((dim(tm), dim(D_MODEL)), in_map),
                  _full(wo_bf.shape), _full(fg.shape)],
        out_specs=pl.BlockSpec((tm, D_MODEL), out_map),
        out_shape=jax.ShapeDtypeStruct((out_rows, D_MODEL), F32),
        compiler_params=_cparams(*(["parallel"] * len(grid))),
        name="tail1",
    )(mix, h, wo_bf, fg)


def kernel(x_prompt, x_sample, cache_sb_k, cache_sb_v, page_table, state_pool, state_C, state_n, state_m, state_conv, meta_tokens, norm_g, final_norm_g, w_in_a, w_out_a, sb_bias, w_pool, pool_scale, w_in_c, b_gate_c, conv_w, conv_b, w_q, w_k, w_v, skip_c, outnorm_g, w_out_c):
    nb, seq, _ = x_prompt.shape
    bs, ts, _ = x_sample.shape
    lp = PAD + N_META + seq
    n_pool, page = cache_sb_k.shape[1], cache_sb_k.shape[2]
    tm = 2 * TILE

    r = jnp.arange(TILE)
    uinc1 = (r[:, None] >= r[None, :]).astype(BF16)
    uinc = jnp.concatenate([uinc1, uinc1], axis=0)
    lower = (r[None, :] <= r[:, None]).astype(BF16)
    sel = (jnp.arange(16)[:, None] == r[None, :]).astype(BF16)

    w_in_a_bf = w_in_a[0].astype(BF16)
    w_out_a_bf = w_out_a[0].astype(BF16)
    w_pool_bf = w_pool[0].astype(BF16)
    ps = pool_scale[0].reshape(1, W_B)
    g0 = norm_g[0].reshape(1, D_MODEL)
    q_scale = DH_A ** -0.5 * LOG2E
    bias2 = sb_bias[0] * LOG2E
    outs0 = [(0, W_A, q_scale, BF16),
             (W_A, W_A, 1.0, F32), (2 * W_A, W_A, 1.0, F32),
             (W_A, W_A, 1.0, BF16), (2 * W_A, W_A, 1.0, BF16),
             (3 * W_A, W_A, 1.0, F32),
             (4 * W_A, W_B, 1.0, F32), (4 * W_A + W_B, W_B, 1.0, F32)]

    meta = jnp.broadcast_to(meta_tokens.astype(x_prompt.dtype)[None], (nb, N_META, D_MODEL))
    h_p = jnp.concatenate([jnp.zeros((nb, PAD, D_MODEL), x_prompt.dtype), meta, x_prompt], axis=1)
    h_p = h_p.reshape(nb * lp, D_MODEL)
    q_bf, k_p, v_p, k_bf, v_bf, ga_p, u_p, gb_p = _norm_proj(h_p, g0, w_in_a_bf, outs0, tm)
    oa_p = _sb_prompt(bias2, q_bf, k_bf, v_bf, uinc1, nb, lp)
    h1_p = _tail0_prompt(oa_p, ga_p, u_p, gb_p, h_p, w_pool_bf, ps, w_out_a_bf, tm, lp)

    def to_tm(a):
        return a.transpose(1, 0, 2).reshape(ts * bs, a.shape[-1])

    def to_sm(a):
        return a.reshape(ts, bs, a.shape[-1]).transpose(1, 0, 2)

    h_s = to_tm(x_sample)
    outs0_s = [(0, W_A, q_scale, F32)] + outs0[1:3] + outs0[5:]
    q_s, k_s, v_s, ga_s, u_s, gb_s = _norm_proj(h_s, g0, w_in_a_bf, outs0_s, tm)
    q_rep = jnp.repeat(to_sm(q_s), H_A, axis=1)
    row_head = jnp.arange(ts * H_A) % H_A
    hm = (row_head[:, None] == (jnp.arange(W_A) // DH_A)[None, :]).astype(F32)
    bias_rows = jnp.broadcast_to(bias2[row_head][:, None], (ts * H_A, TILE))

    def pages_t(c):
        return c.transpose(0, 2, 3, 1).reshape(n_pool, W_A, page)

    oa_s = _sb_sample(page_table, q_rep, to_sm(k_s), to_sm(v_s), hm, bias_rows, uinc,
                      pages_t(cache_sb_k[0]), pages_t(cache_sb_v[0]))
    pool_ext = jnp.concatenate([state_pool[0].transpose(1, 0, 2), u_s.reshape(ts, bs, W_B)], axis=0)
    h1_s = _tail0_sample(to_tm(oa_s), ga_s, pool_ext, gb_s, h_s, w_pool_bf, ps, w_out_a_bf)

    g1 = norm_g[1].reshape(1, D_MODEL)
    w_in_c_bf = w_in_c[0][:, :2 * INNER_C].astype(BF16)
    n_gate = 2 * H_C
    wg_bf = jnp.pad(w_in_c[0][:, 2 * INNER_C:], ((0, 0), (0, TILE - n_gate))).astype(BF16)
    bg = jnp.pad(b_gate_c[0], (0, TILE - n_gate)).reshape(1, TILE)
    cw, cb = conv_w[0], conv_b[0].reshape(1, INNER_C)
    wq_bf, wk_bf, wv_bf = w_q[0].astype(BF16), w_k[0].astype(BF16), w_v[0].astype(BF16)
    og, sk = outnorm_g[0].reshape(1, INNER_C), skip_c[0].reshape(1, INNER_C)
    w_out_c_bf = w_out_c[0].astype(BF16)
    fg = final_norm_g.reshape(1, D_MODEL)

    xm_p, z_p, gate_p = _norm_proj1(h1_p, g1, w_in_c_bf, wg_bf, bg, tm)
    ca_p, q1_p, k1_p, v1_p = _conv_qkv_prompt(xm_p, cw, cb, wq_bf, wk_bf, wv_bf, tm)
    mix_p, C_p, n_p, m_p = _mlstm_prompt(q1_p, k1_p, v1_p, gate_p, ca_p, z_p, og, sk, lower, sel, nb, lp)
    y_prompt = _tail1(mix_p, h1_p, w_out_c_bf, fg, (nb, seq // tm),
                      lambda b, i: (pl.multiple_of(b * lp + TILE + i * tm, TILE), 0),
                      lambda b, i: (b * (seq // tm) + i, 0),
                      nb * seq, tm, element_rows=True)

    xm_s, z_s, gate_s = _norm_proj1(h1_s, g1, w_in_c_bf, wg_bf, bg, tm)
    conv_ext = jnp.concatenate([state_conv[0].transpose(1, 0, 2), xm_s.reshape(ts, bs, INNER_C)], axis=0)
    ca_s, q1_s, k1_s, v1_s = _conv_qkv_sample(conv_ext, cw, cb, wq_bf, wk_bf, wv_bf)
    rp = 16

    def to_rows(a):
        return jnp.pad(to_sm(a), ((0, 0), (0, rp - ts), (0, 0)))

    m0 = jnp.broadcast_to(state_m[0].reshape(bs * H_C, 1, 1), (bs * H_C, 1, TILE))
    mix_s, C_s, n_s, m_s = _mlstm_sample(to_rows(q1_s), to_rows(k1_s), to_rows(v1_s), to_rows(gate_s),
                                         to_rows(ca_s), to_rows(z_s), og, sk,
                                         state_C[0].reshape(bs * H_C, DH_C, DH_C),
                                         state_n[0].reshape(bs * H_C, 1, DH_C), m0, lower[:rp, :rp], sel, ts)
    y_s = _tail1(to_tm(mix_s[:, :ts]), h1_s, w_out_c_bf, fg, (ts * bs // tm,),
                 lambda i: (i, 0), lambda i: (i, 0), ts * bs, tm)

    def heads(a):
        return a.reshape(a.shape[:-1] + (H_A, DH_A))

    k3, v3, u3 = (a.reshape(nb, lp, -1) for a in (k_p, v_p, u_p))
    return (y_prompt.reshape(nb, seq, D_MODEL), to_sm(y_s),
            heads(k3[:, PAD:])[None], heads(v3[:, PAD:])[None],
            heads(to_sm(k_s))[None], heads(to_sm(v_s))[None],
            u3[:, lp - (POOL_MAX - 1):][None], pool_ext[ts:].transpose(1, 0, 2)[None],
            C_p.reshape(1, nb, H_C, DH_C, DH_C), C_s.reshape(1, bs, H_C, DH_C, DH_C),
            n_p.reshape(1, nb, H_C, DH_C), n_s.reshape(1, bs, H_C, DH_C),
            m_p[:, 0, 0].reshape(1, nb, H_C), m_s[:, 0, 0].reshape(1, bs, H_C),
            xm_p.reshape(nb, lp, INNER_C)[:, lp - (CONV_W - 1):][None], conv_ext[ts:].transpose(1, 0, 2)[None])
```

```python
import functools

import jax
import jax.numpy as jnp
from jax import lax
from jax.experimental import pallas as pl
from jax.experimental.pallas import tpu as pltpu

F32 = jnp.float32
BF16 = jnp.bfloat16

D_MODEL = 1024
N_META = 16
W_A = 512
H_A = 8
DH_A = 64
W_B = 512
POOL_WINDOWS = (2, 4, 8, 16)
POOL_GC = 128
POOL_MAX = 16
INNER_C = 2048
H_C = 4
DH_C = 512
CONV_W = 4
EPS = 1e-6

TILE = 128
PAD = TILE - N_META
NEG = -1e30
VMEM_LIMIT = 56 * 1024 * 1024


def _cparams(*sem):
    return pltpu.CompilerParams(dimension_semantics=sem, vmem_limit_bytes=VMEM_LIMIT)


def _silu(x):
    return x / (1.0 + jnp.exp(-x))


def _softplus(z):
    return jnp.maximum(z, 0.0) + jnp.log(1.0 + jnp.exp(-jnp.abs(z)))


def _split2(x):
    hi = x.astype(BF16)
    lo = (x - hi.astype(F32)).astype(BF16)
    return hi, lo


def _split3(x):
    hi = x.astype(BF16)
    r = x - hi.astype(F32)
    mid = r.astype(BF16)
    lo = (r - mid.astype(F32)).astype(BF16)
    return hi, mid, lo


def _dot(a, b):
    return jnp.dot(a, b, preferred_element_type=F32)


def _dot_nt(a, b):
    return lax.dot_general(a, b, (((1,), (1,)), ((), ())), preferred_element_type=F32)


def _dot_tn(a, b):
    return lax.dot_general(a, b, (((0,), (0,)), ((), ())), preferred_element_type=F32)


def _full(shape):
    nd = len(shape)
    return pl.BlockSpec(shape, lambda *_: (0,) * nd)


def _norm_proj_kernel(x_ref, g_ref, w_ref, *out_refs, outs):
    x = x_ref[...]
    ms = jnp.mean(x * x, axis=-1, keepdims=True)
    xn = (x * lax.rsqrt(ms + EPS) * g_ref[...]).astype(BF16)
    accs = {}
    for o_ref, (off, width, scale) in zip(out_refs, outs):
        if (off, width) not in accs:
            accs[(off, width)] = _dot(xn, w_ref[:, off:off + width])
        acc = accs[(off, width)]
        if scale != 1.0:
            acc = acc * scale
        o_ref[...] = acc.astype(o_ref.dtype)


def _norm_proj(x, g, w_bf, outs, tm):
    rows = x.shape[0]
    kern = functools.partial(_norm_proj_kernel, outs=[(o, w, s) for o, w, s, _ in outs])
    return pl.pallas_call(
        kern,
        grid=(rows // tm,),
        in_specs=[pl.BlockSpec((tm, D_MODEL), lambda i: (i, 0)), _full((1, D_MODEL)), _full(w_bf.shape)],
        out_specs=[pl.BlockSpec((tm, w), lambda i: (i, 0)) for _, w, _, _ in outs],
        out_shape=[jax.ShapeDtypeStruct((rows, w), dt) for _, w, _, dt in outs],
        compiler_params=_cparams("parallel"),
        name="norm_proj",
    )(x, g, w_bf)


def _norm_proj1_kernel(x_ref, g_ref, w_ref, wg_ref, bg_ref, xm_ref, z_ref, gate_ref):
    x = x_ref[...]
    ms = jnp.mean(x * x, axis=-1, keepdims=True)
    xn = (x * lax.rsqrt(ms + EPS) * g_ref[...]).astype(BF16)
    xm_ref[...] = _dot(xn, w_ref[:, :INNER_C])
    z_ref[...] = _dot(xn, w_ref[:, INNER_C:])
    gl = _dot(xn, wg_ref[...]) + bg_ref[...]
    lane = lax.broadcasted_iota(jnp.int32, gl.shape, 1)
    gate_ref[...] = jnp.where(lane >= H_C, -_softplus(-gl), gl)


def _norm_proj1(x, g, w_bf, wg_bf, bg, tm):
    rows = x.shape[0]
    return pl.pallas_call(
        _norm_proj1_kernel,
        grid=(rows // tm,),
        in_specs=[pl.BlockSpec((tm, D_MODEL), lambda i: (i, 0)), _full((1, D_MODEL)), _full(w_bf.shape),
                  _full(wg_bf.shape), _full(bg.shape)],
        out_specs=[pl.BlockSpec((tm, INNER_C), lambda i: (i, 0)), pl.BlockSpec((tm, INNER_C), lambda i: (i, 0)),
                   pl.BlockSpec((tm, TILE), lambda i: (i, 0))],
        out_shape=[jax.ShapeDtypeStruct((rows, INNER_C), F32), jax.ShapeDtypeStruct((rows, INNER_C), F32),
                   jax.ShapeDtypeStruct((rows, TILE), F32)],
        compiler_params=_cparams("parallel"),
        name="norm_proj1",
    )(x, g, w_bf, wg_bf, bg)


LOG2E = 1.4426950408889634


def _softplus2(y):
    return jnp.where(y > 64.0, y, jnp.log2(1.0 + jnp.exp2(y)))


def _sb_prompt_kernel(bias_ref, q_ref, k_ref, v_ref, uinc_ref, o_ref, qm_ref):
    qi = pl.program_id(1)
    uinc = uinc_ref[...]
    lane = lax.broadcasted_iota(jnp.int32, (TILE, TILE), 1)
    first_half = lane < DH_A
    causal = lane < lax.broadcasted_iota(jnp.int32, (TILE, TILE), 0)
    heads = range(H_A)
    pairs = range(H_A // 2)
    for p in pairs:
        qp = q_ref[:, p * TILE:(p + 1) * TILE]
        qm_ref[2 * p] = jnp.where(first_half, qp, jnp.zeros_like(qp))
        qm_ref[2 * p + 1] = jnp.where(first_half, jnp.zeros_like(qp), qp)

    def key_tiles(kss, masks, accs, cbs):
        tiles = range(len(kss))
        kps = [[k_ref[pl.ds(ks, TILE), p * TILE:(p + 1) * TILE] for p in pairs] for ks in kss]
        vps = [[v_ref[pl.ds(ks, TILE), p * TILE:(p + 1) * TILE] for p in pairs] for ks in kss]
        qps = [qm_ref[2 * p:2 * p + 2].reshape(2 * TILE, TILE) for p in pairs]
        zps = [[_dot_nt(qps[p], kps[i][p]) for p in pairs] for i in tiles]
        zs, ts = [], []
        group = H_A // 2
        for i in tiles:
            zs.append([zps[i][h // 2][(h % 2) * TILE:(h % 2 + 1) * TILE] + bias_ref[h] for h in heads])
            ts.append([])
            for g0 in range(0, H_A, group):
                drops = [_softplus2(z) for z in zs[i][g0:g0 + group]]
                if masks[i] is not None:
                    drops = [jnp.where(masks[i], d, 0.0) for d in drops]
                tall = _dot(jnp.concatenate([d.astype(BF16) for d in drops], axis=0), uinc)
                ts[i] += [tall[h * TILE:(h + 1) * TILE] for h in range(group)]
        cbs = list(cbs)
        accs = list(accs)
        for i in tiles:
            ws = [jnp.exp2(zs[i][h] - ts[i][h] - cbs[h]) for h in heads]
            if masks[i] is not None:
                ws = [jnp.where(masks[i], a, 0.0) for a in ws]
            cbs = [cbs[h] + jnp.broadcast_to(ts[i][h][:, 0:1], (TILE, TILE)) for h in heads]
            for p in pairs:
                r = _dot(jnp.concatenate([ws[2 * p].astype(BF16), ws[2 * p + 1].astype(BF16)], axis=0), vps[i][p])
                accs[p] = accs[p] + jnp.where(first_half, r[:TILE], r[TILE:])
        return tuple(accs), tuple(cbs)

    zero = jnp.zeros((TILE, TILE), F32)
    carry = ((zero,) * (H_A // 2), (zero,) * H_A)

    def pair(ks, first_mask, carry):
        ks = pl.multiple_of(ks, TILE)
        return key_tiles([ks, pl.multiple_of(ks - TILE, TILE)], [first_mask, None], *carry)

    def quad(ks, carry):
        kss = [pl.multiple_of(ks - i * TILE, TILE) for i in range(4)]
        return key_tiles(kss, [None] * 4, *carry)

    carry = lax.cond(qi == 0, lambda c: key_tiles([0], [causal], *c), lambda c: pair(qi * TILE, causal, c), carry)
    rest = jnp.maximum(qi - 1, 0)
    carry = lax.fori_loop(0, rest // 4, lambda j, c: quad((qi - 2 - 4 * j) * TILE, c), carry)
    left = rest % 4
    carry = lax.cond(left >= 2, lambda c: pair((left - 1) * TILE, None, c), lambda c: c, carry)
    accs, _ = lax.cond(left % 2 == 1, lambda c: key_tiles([0], [None], *c), lambda c: c, carry)
    for p in pairs:
        o_ref[:, p * TILE:(p + 1) * TILE] = accs[p]


def _sb_prompt(bias, q_bf, k_bf, v_bf, uinc, nb, lp):
    nq = lp // TILE
    return pl.pallas_call(
        _sb_prompt_kernel,
        grid=(nb, nq),
        in_specs=[pl.BlockSpec(memory_space=pltpu.SMEM),
                  pl.BlockSpec((TILE, W_A), lambda b, i: (b * nq + i, 0)),
                  pl.BlockSpec((lp, W_A), lambda b, i: (b, 0)),
                  pl.BlockSpec((lp, W_A), lambda b, i: (b, 0)),
                  _full(uinc.shape)],
        out_specs=pl.BlockSpec((TILE, W_A), lambda b, i: (b * nq + i, 0)),
        out_shape=jax.ShapeDtypeStruct((nb * lp, W_A), F32),
        scratch_shapes=[pltpu.VMEM((H_A, TILE, TILE), BF16)],
        compiler_params=_cparams("parallel", "arbitrary"),
        name="sb_prompt",
    )(bias, q_bf, k_bf, v_bf, uinc)


def _sb_sample_kernel(pt_ref, q_ref, kn_ref, vn_ref, hm_ref, bias_ref, uinc_ref, *rest, n_pages, ts):
    kp_refs = rest[:n_pages]
    vp_refs = rest[n_pages:2 * n_pages]
    o_ref = rest[2 * n_pages]
    hm = hm_ref[...]
    qbd = q_ref[...] * hm
    bias = bias_ref[...]
    uinc = uinc_ref[...]
    rows = ts * H_A
    t_of_row = lax.broadcasted_iota(jnp.int32, (rows, 1), 0) // H_A
    out = jnp.zeros((rows, W_A), F32)
    carry = jnp.zeros((rows, 1), F32)
    for j in range(ts - 1, -1, -1):
        z = jnp.sum(qbd * kn_ref[j:j + 1, :], axis=-1, keepdims=True) + bias[:, 0:1]
        sp = _softplus2(z)
        m = j < t_of_row
        a = jnp.where(m, jnp.exp2(z - sp - carry), 0.0)
        carry = carry + jnp.where(m, sp, 0.0)
        out = out + a * vn_ref[j:j + 1, :]
    cb = jnp.broadcast_to(carry, (rows, TILE))
    qbd_bf = qbd.astype(BF16)
    pages = range(n_pages - 1, -1, -1)
    zs = [_dot(qbd_bf, kp_refs[p][...].astype(BF16)) + bias for p in pages]
    sps = [_softplus2(z) for z in zs]
    tall = _dot(jnp.concatenate([jnp.concatenate(_split2(sp), axis=1) for sp in sps], axis=0), uinc)
    for i, p in enumerate(pages):
        tail = tall[i * rows:(i + 1) * rows]
        a = jnp.exp2(zs[i] - tail - cb)
        cb = cb + jnp.broadcast_to(tail[:, 0:1], (rows, TILE))
        out = out + _dot_nt(a.astype(BF16), vp_refs[p][...].astype(BF16))
    o_ref[...] = jnp.sum((out * hm).reshape(ts, H_A, W_A), axis=1)


def _sb_sample(page_table, q_rep, kn, vn, hm, bias_rows, uinc, cache_k, cache_v):
    bs, n_pages = page_table.shape
    ts = kn.shape[1]
    page = cache_k.shape[2]
    rows = ts * H_A

    def page_spec(p):
        return pl.BlockSpec((None, W_A, page), lambda s, pt: (pt[s * n_pages + p], 0, 0))

    grid_spec = pltpu.PrefetchScalarGridSpec(
        num_scalar_prefetch=1,
        grid=(bs,),
        in_specs=[pl.BlockSpec((None, rows, W_A), lambda s, pt: (s, 0, 0)),
                  pl.BlockSpec((None, ts, W_A), lambda s, pt: (s, 0, 0)),
                  pl.BlockSpec((None, ts, W_A), lambda s, pt: (s, 0, 0)),
                  pl.BlockSpec(hm.shape, lambda s, pt: (0, 0)),
                  pl.BlockSpec(bias_rows.shape, lambda s, pt: (0, 0)),
                  pl.BlockSpec(uinc.shape, lambda s, pt: (0, 0))]
                 + [page_spec(p) for p in range(n_pages)] * 2,
        out_specs=pl.BlockSpec((None, ts, W_A), lambda s, pt: (s, 0, 0)),
    )
    return pl.pallas_call(
        functools.partial(_sb_sample_kernel, n_pages=n_pages, ts=ts),
        grid_spec=grid_spec,
        out_shape=jax.ShapeDtypeStruct((bs, ts, W_A), F32),
        compiler_params=_cparams("arbitrary"),
        name="sb_sample",
    )(page_table.reshape(-1), q_rep, kn, vn, hm, bias_rows, uinc,
      *([cache_k] * n_pages), *([cache_v] * n_pages))


def _tail0_math(o_a, g_a, u, win_sums, cnts, g_b, h, wp_ref, ps, wo_ref):
    parts = [(o_a * _silu(g_a)).astype(BF16)]
    for g in range(len(POOL_WINDOWS)):
        cs = slice(g * POOL_GC, (g + 1) * POOL_GC)
        diff = win_sums[g] / cnts[g] - u[:, cs]
        y = _dot(diff.astype(BF16), wp_ref[g]) * ps[:, cs]
        parts.append((y * _silu(g_b[:, cs])).astype(BF16))
    mixed = jnp.concatenate(parts, axis=1)
    return h + _dot(mixed, wo_ref[...])


def _tail0_prompt_kernel(oa_ref, ga_ref, u_ref, halo_ref, gb_ref, h_ref, wp_ref, ps_ref, wo_ref, out_ref,
                         ext_ref, *, tm, lp):
    i = pl.program_id(0)
    ext_ref[0:POOL_MAX, :] = halo_ref[...]
    ext_ref[POOL_MAX:POOL_MAX + tm, :] = u_ref[...]
    row = (i * tm) % lp + lax.broadcasted_iota(jnp.int32, (tm, 1), 0)
    pos = jnp.where(row >= lp, row - lp, row) - PAD
    win_sums, cnts = [], []
    for g, w in enumerate(POOL_WINDOWS):
        cs = slice(g * POOL_GC, (g + 1) * POOL_GC)
        s = ext_ref[POOL_MAX:POOL_MAX + tm, cs]
        for j in range(1, w):
            s = s + ext_ref[POOL_MAX - j:POOL_MAX - j + tm, cs]
        win_sums.append(s)
        cnts.append(jnp.clip(pos + 1, 1, w).astype(F32))
    out_ref[...] = _tail0_math(oa_ref[...], ga_ref[...], u_ref[...], win_sums, cnts, gb_ref[...], h_ref[...],
                               wp_ref, ps_ref[...], wo_ref)


def _tail0_prompt(o_a, g_a, u, g_b, h, wp_bf, ps, wo_bf, tm, lp):
    rows = h.shape[0]
    hb = tm // POOL_MAX
    row = lambda i: (i, 0)
    return pl.pallas_call(
        functools.partial(_tail0_prompt_kernel, tm=tm, lp=lp),
        grid=(rows // tm,),
        in_specs=[pl.BlockSpec((tm, W_A), row), pl.BlockSpec((tm, W_A), row), pl.BlockSpec((tm, W_B), row),
                  pl.BlockSpec((POOL_MAX, W_B), lambda i: (jnp.maximum(i * hb - 1, 0), 0)),
                  pl.BlockSpec((tm, W_B), row), pl.BlockSpec((tm, D_MODEL), row),
                  _full(wp_bf.shape), _full(ps.shape), _full(wo_bf.shape)],
        out_specs=pl.BlockSpec((tm, D_MODEL), row),
        out_shape=jax.ShapeDtypeStruct((rows, D_MODEL), F32),
        scratch_shapes=[pltpu.VMEM((POOL_MAX + tm, W_B), F32)],
        compiler_params=_cparams("parallel"),
        name="tail0_prompt",
    )(o_a, g_a, u, u, g_b, h, wp_bf, ps, wo_bf)


def _tail0_sample_kernel(oa_ref, ga_ref, ext_ref, gb_ref, h_ref, wp_ref, ps_ref, wo_ref, out_ref, *, ts, bs):
    pre = POOL_MAX - 1
    u = jnp.concatenate([ext_ref[pre + t] for t in range(ts)], axis=0)
    win_sums, cnts = [], []
    for g, w in enumerate(POOL_WINDOWS):
        cs = slice(g * POOL_GC, (g + 1) * POOL_GC)
        per_t = []
        for t in range(ts):
            s = ext_ref[pre + t, :, cs]
            for j in range(1, w):
                s = s + ext_ref[pre + t - j, :, cs]
            per_t.append(s)
        win_sums.append(jnp.concatenate(per_t, axis=0))
        cnts.append(float(w))
    out_ref[...] = _tail0_math(oa_ref[...], ga_ref[...], u, win_sums, cnts, gb_ref[...], h_ref[...],
                               wp_ref, ps_ref[...], wo_ref)


def _tail0_sample(o_a, g_a, ext_t, g_b, h, wp_bf, ps, wo_bf):
    rows = h.shape[0]
    ts = ext_t.shape[0] - (POOL_MAX - 1)
    args = (o_a, g_a, ext_t, g_b, h, wp_bf, ps, wo_bf)
    return pl.pallas_call(
        functools.partial(_tail0_sample_kernel, ts=ts, bs=ext_t.shape[1]),
        grid=(1,),
        in_specs=[_full(a.shape) for a in args],
        out_specs=_full((rows, D_MODEL)),
        out_shape=jax.ShapeDtypeStruct((rows, D_MODEL), F32),
        compiler_params=_cparams("arbitrary"),
        name="tail0_sample",
    )(*args)


def _qkv_math(ca, xm, wq_ref, wk_ref, wv_ref, q_ref, k_ref, v_ref):
    ca_bf = ca.astype(BF16)
    xm_bf = xm.astype(BF16)
    for h in range(H_C):
        cs = slice(h * DH_C, (h + 1) * DH_C)
        q_ref[:, cs] = _dot(ca_bf[:, cs], wq_ref[h]).astype(BF16)
        k_ref[:, cs] = (_dot(ca_bf[:, cs], wk_ref[h]) * (DH_C ** -0.5)).astype(BF16)
        v_ref[:, cs] = _dot(xm_bf[:, cs], wv_ref[h]).astype(BF16)


def _conv_qkv_prompt_kernel(xm_ref, halo_ref, cw_ref, cb_ref, wq_ref, wk_ref, wv_ref,
                            ca_ref, q_ref, k_ref, v_ref, ext_ref, *, tm):
    ext_ref[0:8, :] = halo_ref[...]
    ext_ref[8:8 + tm, :] = xm_ref[...]
    conv = jnp.broadcast_to(cb_ref[...], (tm, INNER_C))
    for j in range(CONV_W):
        s = 8 - (CONV_W - 1) + j
        conv = conv + ext_ref[s:s + tm, :] * cw_ref[j:j + 1, :]
    ca = _silu(conv)
    ca_ref[...] = ca
    _qkv_math(ca, xm_ref[...], wq_ref, wk_ref, wv_ref, q_ref, k_ref, v_ref)


def _conv_qkv_prompt(xm, cw, cb, wq_bf, wk_bf, wv_bf, tm):
    rows = xm.shape[0]
    row = lambda i: (i, 0)
    return pl.pallas_call(
        functools.partial(_conv_qkv_prompt_kernel, tm=tm),
        grid=(rows // tm,),
        in_specs=[pl.BlockSpec((tm, INNER_C), row),
                  pl.BlockSpec((8, INNER_C), lambda i: (jnp.maximum(i * (tm // 8) - 1, 0), 0)),
                  _full(cw.shape), _full(cb.shape), _full(wq_bf.shape), _full(wk_bf.shape), _full(wv_bf.shape)],
        out_specs=[pl.BlockSpec((tm, INNER_C), row)] * 4,
        out_shape=[jax.ShapeDtypeStruct((rows, INNER_C), F32)] + [jax.ShapeDtypeStruct((rows, INNER_C), BF16)] * 3,
        scratch_shapes=[pltpu.VMEM((8 + tm, INNER_C), F32)],
        compiler_params=_cparams("parallel"),
        name="conv_qkv_prompt",
    )(xm, xm, cw, cb, wq_bf, wk_bf, wv_bf)


def _conv_qkv_sample_kernel(ext_ref, cw_ref, cb_ref, wq_ref, wk_ref, wv_ref, ca_ref, q_ref, k_ref, v_ref):
    t = pl.program_id(0)
    bs = ext_ref.shape[1]
    conv = jnp.broadcast_to(cb_ref[...], (bs, INNER_C))
    for j in range(CONV_W):
        conv = conv + ext_ref[t + j] * cw_ref[j:j + 1, :]
    ca = _silu(conv)
    ca_ref[...] = ca
    _qkv_math(ca, ext_ref[t + CONV_W - 1], wq_ref, wk_ref, wv_ref, q_ref, k_ref, v_ref)


def _conv_qkv_sample(ext_t, cw, cb, wq_bf, wk_bf, wv_bf):
    ts = ext_t.shape[0] - (CONV_W - 1)
    bs = ext_t.shape[1]
    row = lambda t: (t, 0)
    return pl.pallas_call(
        _conv_qkv_sample_kernel,
        grid=(ts,),
        in_specs=[_full(ext_t.shape), _full(cw.shape), _full(cb.shape), _full(wq_bf.shape), _full(wk_bf.shape),
                  _full(wv_bf.shape)],
        out_specs=[pl.BlockSpec((bs, INNER_C), row)] * 4,
        out_shape=[jax.ShapeDtypeStruct((ts * bs, INNER_C), F32)]
                  + [jax.ShapeDtypeStruct((ts * bs, INNER_C), BF16)] * 3,
        compiler_params=_cparams("arbitrary"),
        name="conv_qkv_sample",
    )(ext_t, cw, cb, wq_bf, wk_bf, wv_bf)


def _mlstm_heads(qs, ks, vs, g, valid, Cs, ns, ms, lower, sel):
    nh = len(qs)
    H = range(nh)
    T = qs[0].shape[0]
    lane = lax.broadcasted_iota(jnp.int32, g.shape, 1)
    g = jnp.where(valid, g, jnp.where(lane < nh, NEG, 0.0))
    qks = [_dot_nt(qs[h], ks[h]) for h in H]
    qcs = [_dot(qs[h], Cs[h].astype(BF16)) for h in H]
    g_parts = _split3(g)
    cum = sum(_dot(lower, p) for p in g_parts)
    rows_g = sum(_dot_nt(sel, p) for p in g_parts)
    rows_c = sum(_dot_nt(sel, p) for p in _split3(cum))
    tri = lax.broadcasted_iota(jnp.int32, (T, T), 1) <= lax.broadcasted_iota(jnp.int32, (T, T), 0)
    b_cs = [cum[:, nh + h:nh + h + 1] for h in H]
    ds = [jnp.where(tri, b_cs[h] - rows_c[nh + h:nh + h + 1, :] + rows_g[h:h + 1, :], NEG) for h in H]
    m_inters = [b_cs[h] + ms[h] for h in H]
    m_ts = [jnp.maximum(m_inters[h], jnp.max(ds[h], axis=-1, keepdims=True)) for h in H]
    ss = [jnp.where(tri, qks[h] * jnp.exp(ds[h] - m_ts[h]), 0.0) for h in H]
    svs = [_dot(ss[h].astype(BF16), vs[h]) for h in H]
    m_news = [m_ts[h][T - 1:T, :] for h in H]
    b_lasts = [b_cs[h][T - 1:T, :] for h in H]
    kws = [ks[h].astype(F32) * jnp.exp(b_lasts[h] - b_cs[h] + g[:, h:h + 1] - m_news[h]) for h in H]
    upds = [_dot_tn(kws[h].astype(BF16), vs[h]) for h in H]
    decays = [jnp.exp(b_lasts[h] + ms[h] - m_news[h]) for h in H]
    C_news = [decays[h] * Cs[h] + upds[h] for h in H]
    n_news = [decays[h] * ns[h] + jnp.sum(kws[h], axis=0, keepdims=True) for h in H]
    hns = []
    for h in H:
        w_inter = jnp.exp(m_inters[h] - m_ts[h])
        num = w_inter * qcs[h] + svs[h]
        den = (w_inter * jnp.sum(qs[h].astype(F32) * ns[h], axis=-1, keepdims=True)
               + jnp.sum(ss[h], axis=-1, keepdims=True))
        hh = num / jnp.maximum(jnp.abs(den), jnp.exp(-m_ts[h]))
        hc = hh - jnp.mean(hh, axis=-1, keepdims=True)
        var = jnp.mean(hc * hc, axis=-1, keepdims=True)
        hns.append(hc * lax.rsqrt(var + EPS))
    return hns, C_news, n_news, m_news


def _mlstm_step(q_ref, k_ref, v_ref, g_ref, ca_ref, z_ref, og_ref, sk_ref, valid, c_in, n_in, m_in,
                lower_ref, sel_ref, mix_ref, c_ref, n_ref, m_ref):
    H = range(H_C)
    cols = [slice(h * DH_C, (h + 1) * DH_C) for h in H]
    hns, Cs, ns, ms = _mlstm_heads(
        [q_ref[:, c] for c in cols], [k_ref[:, c] for c in cols], [v_ref[:, c] for c in cols], g_ref[...], valid,
        [c_in[h] for h in H], [n_in[h] for h in H], [m_in[h][:, 0:1] for h in H], lower_ref[...], sel_ref[...])
    for h in H:
        c = cols[h]
        mix = (hns[h] * og_ref[:, c] + sk_ref[:, c] * ca_ref[:, c]) * _silu(z_ref[:, c])
        mix_ref[:, c] = mix.astype(mix_ref.dtype)
        c_ref[h] = Cs[h]
        n_ref[h] = ns[h]
        m_ref[h] = jnp.broadcast_to(ms[h], (1, TILE))


def _mlstm_prompt_kernel(q_ref, k_ref, v_ref, g_ref, ca_ref, z_ref, og_ref, sk_ref, lower_ref, sel_ref,
                         mix_ref, c_ref, n_ref, m_ref):
    c = pl.program_id(1)

    @pl.when(c == 0)
    def _():
        c_ref[...] = jnp.zeros_like(c_ref)
        n_ref[...] = jnp.zeros_like(n_ref)
        m_ref[...] = jnp.zeros_like(m_ref)

    valid = c * TILE + lax.broadcasted_iota(jnp.int32, (TILE, 1), 0) >= PAD
    _mlstm_step(q_ref, k_ref, v_ref, g_ref, ca_ref, z_ref, og_ref, sk_ref, valid, c_ref, n_ref, m_ref,
                lower_ref, sel_ref, mix_ref, c_ref, n_ref, m_ref)


def _mlstm_prompt(q, k, v, g, ca, z, og, sk, lower, sel, nb, lp):
    nc = lp // TILE
    blk = lambda b, c: (b * nc + c, 0)
    st = lambda b, c: (b, 0, 0)
    return pl.pallas_call(
        _mlstm_prompt_kernel,
        grid=(nb, nc),
        in_specs=[pl.BlockSpec((TILE, INNER_C), blk)] * 3
                 + [pl.BlockSpec((TILE, TILE), blk), pl.BlockSpec((TILE, INNER_C), blk),
                    pl.BlockSpec((TILE, INNER_C), blk), _full(og.shape), _full(sk.shape),
                    _full(lower.shape), _full(sel.shape)],
        out_specs=[pl.BlockSpec((TILE, INNER_C), blk), pl.BlockSpec((H_C, DH_C, DH_C), st),
                   pl.BlockSpec((H_C, 1, DH_C), st), pl.BlockSpec((H_C, 1, TILE), st)],
        out_shape=[jax.ShapeDtypeStruct((nb * lp, INNER_C), BF16),
                   jax.ShapeDtypeStruct((nb * H_C, DH_C, DH_C), F32),
                   jax.ShapeDtypeStruct((nb * H_C, 1, DH_C), F32),
                   jax.ShapeDtypeStruct((nb * H_C, 1, TILE), F32)],
        compiler_params=_cparams("parallel", "arbitrary"),
        name="mlstm_prompt",
    )(q, k, v, g, ca, z, og, sk, lower, sel)


def _mlstm_sample_kernel(q_ref, k_ref, v_ref, g_ref, ca_ref, z_ref, og_ref, sk_ref, c_in, n_in, m_in,
                         lower_ref, sel_ref, mix_ref, c_ref, n_ref, m_ref, *, ts):
    valid = lax.broadcasted_iota(jnp.int32, (q_ref.shape[0], 1), 0) < ts
    _mlstm_step(q_ref, k_ref, v_ref, g_ref, ca_ref, z_ref, og_ref, sk_ref, valid, c_in, n_in, m_in,
                lower_ref, sel_ref, mix_ref, c_ref, n_ref, m_ref)


def _mlstm_sample(q, k, v, g, ca, z, og, sk, c0, n0, m0, lower, sel, ts):
    bs, rp = q.shape[0], q.shape[1]
    blk = lambda s: (s, 0, 0)
    rows = pl.BlockSpec((None, rp, INNER_C), blk)
    state_specs = [pl.BlockSpec((H_C, DH_C, DH_C), blk), pl.BlockSpec((H_C, 1, DH_C), blk),
                   pl.BlockSpec((H_C, 1, TILE), blk)]
    return pl.pallas_call(
        functools.partial(_mlstm_sample_kernel, ts=ts),
        grid=(bs,),
        in_specs=[rows] * 3 + [pl.BlockSpec((None, rp, TILE), blk), rows, rows,
                               _full(og.shape), _full(sk.shape)]
                 + state_specs + [_full(lower.shape), _full(sel.shape)],
        out_specs=[rows] + state_specs,
        out_shape=[jax.ShapeDtypeStruct((bs, rp, INNER_C), BF16),
                   jax.ShapeDtypeStruct((bs * H_C, DH_C, DH_C), F32),
                   jax.ShapeDtypeStruct((bs * H_C, 1, DH_C), F32),
                   jax.ShapeDtypeStruct((bs * H_C, 1, TILE), F32)],
        compiler_params=_cparams("parallel"),
        name="mlstm_sample",
    )(q, k, v, g, ca, z, og, sk, c0, n0, m0, lower, sel)


def _tail1_kernel(mix_ref, h_ref, wo_ref, fg_ref, y_ref):
    h2 = h_ref[...] + _dot(mix_ref[...], wo_ref[...])
    ms = jnp.mean(h2 * h2, axis=-1, keepdims=True)
    y_ref[...] = h2 * lax.rsqrt(ms + EPS) * fg_ref[...]


def _tail1(mix, h, wo_bf, fg, grid, in_map, out_map, out_rows, tm, element_rows=False):
    dim = pl.Element if element_rows else (lambda n: n)
    return pl.pallas_call(
        _tail1_kernel,
        grid=grid,
        in_specs=[pl.BlockSpec((dim(tm), dim(INNER_C)), in_map), pl.BlockSpec((dim(tm), dim(D_MODEL)), in_map),
                  _full(wo_bf.shape), _full(fg.shape)],
        out_specs=pl.BlockSpec((tm, D_MODEL), out_map),
        out_shape=jax.ShapeDtypeStruct((out_rows, D_MODEL), F32),
        compiler_params=_cparams(*(["parallel"] * len(grid))),
        name="tail1",
    )(mix, h, wo_bf, fg)


def kernel(x_prompt, x_sample, cache_sb_k, cache_sb_v, page_table, state_pool, state_C, state_n, state_m, state_conv, meta_tokens, norm_g, final_norm_g, w_in_a, w_out_a, sb_bias, w_pool, pool_scale, w_in_c, b_gate_c, conv_w, conv_b, w_q, w_k, w_v, skip_c, outnorm_g, w_out_c):
    nb, seq, _ = x_prompt.shape
    bs, ts, _ = x_sample.shape
    lp = PAD + N_META + seq
    n_pool, page = cache_sb_k.shape[1], cache_sb_k.shape[2]
    tm = 4 * TILE

    r = jnp.arange(TILE)
    uinc1 = (r[:, None] >= r[None, :]).astype(BF16)
    uinc = jnp.concatenate([uinc1, uinc1], axis=0)
    lower = (r[None, :] <= r[:, None]).astype(BF16)
    sel = (jnp.arange(16)[:, None] == r[None, :]).astype(BF16)

    w_in_a_bf = w_in_a[0].astype(BF16)
    w_out_a_bf = w_out_a[0].astype(BF16)
    w_pool_bf = w_pool[0].astype(BF16)
    ps = pool_scale[0].reshape(1, W_B)
    g0 = norm_g[0].reshape(1, D_MODEL)
    q_scale = DH_A ** -0.5 * LOG2E
    bias2 = sb_bias[0] * LOG2E
    outs0 = [(0, W_A, q_scale, BF16),
             (W_A, W_A, 1.0, F32), (2 * W_A, W_A, 1.0, F32),
             (W_A, W_A, 1.0, BF16), (2 * W_A, W_A, 1.0, BF16),
             (3 * W_A, W_A, 1.0, F32),
             (4 * W_A, W_B, 1.0, F32), (4 * W_A + W_B, W_B, 1.0, F32)]

    meta = jnp.broadcast_to(meta_tokens.astype(x_prompt.dtype)[None], (nb, N_META, D_MODEL))
    h_p = jnp.concatenate([jnp.zeros((nb, PAD, D_MODEL), x_prompt.dtype), meta, x_prompt], axis=1)
    h_p = h_p.reshape(nb * lp, D_MODEL)
    q_bf, k_p, v_p, k_bf, v_bf, ga_p, u_p, gb_p = _norm_proj(h_p, g0, w_in_a_bf, outs0, tm)
    oa_p = _sb_prompt(bias2, q_bf, k_bf, v_bf, uinc1, nb, lp)
    h1_p = _tail0_prompt(oa_p, ga_p, u_p, gb_p, h_p, w_pool_bf, ps, w_out_a_bf, tm, lp)

    def to_tm(a):
        return a.transpose(1, 0, 2).reshape(ts * bs, a.shape[-1])

    def to_sm(a):
        return a.reshape(ts, bs, a.shape[-1]).transpose(1, 0, 2)

    h_s = to_tm(x_sample)
    outs0_s = [(0, W_A, q_scale, F32)] + outs0[1:3] + outs0[5:]
    q_s, k_s, v_s, ga_s, u_s, gb_s = _norm_proj(h_s, g0, w_in_a_bf, outs0_s, tm)
    q_rep = jnp.repeat(to_sm(q_s), H_A, axis=1)
    row_head = jnp.arange(ts * H_A) % H_A
    hm = (row_head[:, None] == (jnp.arange(W_A) // DH_A)[None, :]).astype(F32)
    bias_rows = jnp.broadcast_to(bias2[row_head][:, None], (ts * H_A, TILE))

    def pages_t(c):
        return c.transpose(0, 2, 3, 1).reshape(n_pool, W_A, page)

    oa_s = _sb_sample(page_table, q_rep, to_sm(k_s), to_sm(v_s), hm, bias_rows, uinc,
                      pages_t(cache_sb_k[0]), pages_t(cache_sb_v[0]))
    pool_ext = jnp.concatenate([state_pool[0].transpose(1, 0, 2), u_s.reshape(ts, bs, W_B)], axis=0)
    h1_s = _tail0_sample(to_tm(oa_s), ga_s, pool_ext, gb_s, h_s, w_pool_bf, ps, w_out_a_bf)

    g1 = norm_g[1].reshape(1, D_MODEL)
    w_in_c_bf = w_in_c[0][:, :2 * INNER_C].astype(BF16)
    n_gate = 2 * H_C
    wg_bf = jnp.pad(w_in_c[0][:, 2 * INNER_C:], ((0, 0), (0, TILE - n_gate))).astype(BF16)
    bg = jnp.pad(b_gate_c[0], (0, TILE - n_gate)).reshape(1, TILE)
    cw, cb = conv_w[0], conv_b[0].reshape(1, INNER_C)
    wq_bf, wk_bf, wv_bf = w_q[0].astype(BF16), w_k[0].astype(BF16), w_v[0].astype(BF16)
    og, sk = outnorm_g[0].reshape(1, INNER_C), skip_c[0].reshape(1, INNER_C)
    w_out_c_bf = w_out_c[0].astype(BF16)
    fg = final_norm_g.reshape(1, D_MODEL)

    xm_p, z_p, gate_p = _norm_proj1(h1_p, g1, w_in_c_bf, wg_bf, bg, tm)
    ca_p, q1_p, k1_p, v1_p = _conv_qkv_prompt(xm_p, cw, cb, wq_bf, wk_bf, wv_bf, tm)
    mix_p, C_p, n_p, m_p = _mlstm_prompt(q1_p, k1_p, v1_p, gate_p, ca_p, z_p, og, sk, lower, sel, nb, lp)
    y_prompt = _tail1(mix_p, h1_p, w_out_c_bf, fg, (nb, seq // tm),
                      lambda b, i: (pl.multiple_of(b * lp + TILE + i * tm, TILE), 0),
                      lambda b, i: (b * (seq // tm) + i, 0),
                      nb * seq, tm, element_rows=True)

    xm_s, z_s, gate_s = _norm_proj1(h1_s, g1, w_in_c_bf, wg_bf, bg, tm)
    conv_ext = jnp.concatenate([state_conv[0].transpose(1, 0, 2), xm_s.reshape(ts, bs, INNER_C)], axis=0)
    ca_s, q1_s, k1_s, v1_s = _conv_qkv_sample(conv_ext, cw, cb, wq_bf, wk_bf, wv_bf)
    rp = 16

    def to_rows(a):
        return jnp.pad(to_sm(a), ((0, 0), (0, rp - ts), (0, 0)))

    m0 = jnp.broadcast_to(state_m[0].reshape(bs * H_C, 1, 1), (bs * H_C, 1, TILE))
    mix_s, C_s, n_s, m_s = _mlstm_sample(to_rows(q1_s), to_rows(k1_s), to_rows(v1_s), to_rows(gate_s),
                                         to_rows(ca_s), to_rows(z_s), og, sk,
                                         state_C[0].reshape(bs * H_C, DH_C, DH_C),
                                         state_n[0].reshape(bs * H_C, 1, DH_C), m0, lower[:rp, :rp], sel, ts)
    y_s = _tail1(to_tm(mix_s[:, :ts]), h1_s, w_out_c_bf, fg, (ts * bs // tm,),
                 lambda i: (i, 0), lambda i: (i, 0), ts * bs, tm)

    def heads(a):
        return a.reshape(a.shape[:-1] + (H_A, DH_A))

    k3, v3, u3 = (a.reshape(nb, lp, -1) for a in (k_p, v_p, u_p))
    return (y_prompt.reshape(nb, seq, D_MODEL), to_sm(y_s),
            heads(k3[:, PAD:])[None], heads(v3[:, PAD:])[None],
            heads(to_sm(k_s))[None], heads(to_sm(v_s))[None],
            u3[:, lp - (POOL_MAX - 1):][None], pool_ext[ts:].transpose(1, 0, 2)[None],
            C_p.reshape(1, nb, H_C, DH_C, DH_C), C_s.reshape(1, bs, H_C, DH_C, DH_C),
            n_p.reshape(1, nb, H_C, DH_C), n_s.reshape(1, bs, H_C, DH_C),
            m_p[:, 0, 0].reshape(1, nb, H_C), m_s[:, 0, 0].reshape(1, bs, H_C),
            xm_p.reshape(nb, lp, INNER_C)[:, lp - (CONV_W - 1):][None], conv_ext[ts:].transpose(1, 0, 2)[None])
```

```python
import functools

import jax
import jax.numpy as jnp
from jax import lax
from jax.experimental import pallas as pl
from jax.experimental.pallas import tpu as pltpu

F32 = jnp.float32
BF16 = jnp.bfloat16

D_MODEL = 1024
N_META = 16
W_A = 512
H_A = 8
DH_A = 64
W_B = 512
POOL_WINDOWS = (2, 4, 8, 16)
POOL_GC = 128
POOL_MAX = 16
INNER_C = 2048
H_C = 4
DH_C = 512
CONV_W = 4
EPS = 1e-6

TILE = 128
PAD = TILE - N_META
NEG = -1e30
VMEM_LIMIT = 56 * 1024 * 1024


def _cparams(*sem):
    return pltpu.CompilerParams(dimension_semantics=sem, vmem_limit_bytes=VMEM_LIMIT)


def _silu(x):
    return x / (1.0 + jnp.exp(-x))


def _softplus(z):
    return jnp.maximum(z, 0.0) + jnp.log(1.0 + jnp.exp(-jnp.abs(z)))


def _split2(x):
    hi = x.astype(BF16)
    lo = (x - hi.astype(F32)).astype(BF16)
    return hi, lo


def _split3(x):
    hi = x.astype(BF16)
    r = x - hi.astype(F32)
    mid = r.astype(BF16)
    lo = (r - mid.astype(F32)).astype(BF16)
    return hi, mid, lo


def _dot(a, b):
    return jnp.dot(a, b, preferred_element_type=F32)


def _dot_nt(a, b):
    return lax.dot_general(a, b, (((1,), (1,)), ((), ())), preferred_element_type=F32)


def _dot_tn(a, b):
    return lax.dot_general(a, b, (((0,), (0,)), ((), ())), preferred_element_type=F32)


def _full(shape):
    nd = len(shape)
    return pl.BlockSpec(shape, lambda *_: (0,) * nd)


def _norm_proj_kernel(x_ref, g_ref, w_ref, *out_refs, outs):
    x = x_ref[...]
    ms = jnp.mean(x * x, axis=-1, keepdims=True)
    xn = (x * lax.rsqrt(ms + EPS) * g_ref[...]).astype(BF16)
    accs = {}
    for o_ref, (off, width, scale) in zip(out_refs, outs):
        if (off, width) not in accs:
            accs[(off, width)] = _dot(xn, w_ref[:, off:off + width])
        acc = accs[(off, width)]
        if scale != 1.0:
            acc = acc * scale
        o_ref[...] = acc.astype(o_ref.dtype)


def _norm_proj(x, g, w_bf, outs, tm):
    rows = x.shape[0]
    kern = functools.partial(_norm_proj_kernel, outs=[(o, w, s) for o, w, s, _ in outs])
    return pl.pallas_call(
        kern,
        grid=(rows // tm,),
        in_specs=[pl.BlockSpec((tm, D_MODEL), lambda i: (i, 0)), _full((1, D_MODEL)), _full(w_bf.shape)],
        out_specs=[pl.BlockSpec((tm, w), lambda i: (i, 0)) for _, w, _, _ in outs],
        out_shape=[jax.ShapeDtypeStruct((rows, w), dt) for _, w, _, dt in outs],
        compiler_params=_cparams("parallel"),
        name="norm_proj",
    )(x, g, w_bf)


def _norm_proj1_kernel(x_ref, g_ref, w_ref, wg_ref, bg_ref, xm_ref, z_ref, gate_ref):
    x = x_ref[...]
    ms = jnp.mean(x * x, axis=-1, keepdims=True)
    xn = (x * lax.rsqrt(ms + EPS) * g_ref[...]).astype(BF16)
    xm_ref[...] = _dot(xn, w_ref[:, :INNER_C])
    z_ref[...] = _dot(xn, w_ref[:, INNER_C:])
    gl = _dot(xn, wg_ref[...]) + bg_ref[...]
    lane = lax.broadcasted_iota(jnp.int32, gl.shape, 1)
    gate_ref[...] = jnp.where(lane >= H_C, -_softplus(-gl), gl)


def _norm_proj1(x, g, w_bf, wg_bf, bg, tm):
    rows = x.shape[0]
    return pl.pallas_call(
        _norm_proj1_kernel,
        grid=(rows // tm,),
        in_specs=[pl.BlockSpec((tm, D_MODEL), lambda i: (i, 0)), _full((1, D_MODEL)), _full(w_bf.shape),
                  _full(wg_bf.shape), _full(bg.shape)],
        out_specs=[pl.BlockSpec((tm, INNER_C), lambda i: (i, 0)), pl.BlockSpec((tm, INNER_C), lambda i: (i, 0)),
                   pl.BlockSpec((tm, TILE), lambda i: (i, 0))],
        out_shape=[jax.ShapeDtypeStruct((rows, INNER_C), F32), jax.ShapeDtypeStruct((rows, INNER_C), F32),
                   jax.ShapeDtypeStruct((rows, TILE), F32)],
        compiler_params=_cparams("parallel"),
        name="norm_proj1",
    )(x, g, w_bf, wg_bf, bg)


LOG2E = 1.4426950408889634


def _softplus2(y):
    return jnp.where(y > 64.0, y, jnp.log2(1.0 + jnp.exp2(y)))


def _sb_prompt_kernel(bias_ref, q_ref, k_ref, v_ref, uinc_ref, o_ref, qm_ref):
    qi = pl.program_id(1)
    uinc = uinc_ref[...]
    lane = lax.broadcasted_iota(jnp.int32, (TILE, TILE), 1)
    first_half = lane < DH_A
    causal = lane < lax.broadcasted_iota(jnp.int32, (TILE, TILE), 0)
    heads = range(H_A)
    pairs = range(H_A // 2)
    for p in pairs:
        qp = q_ref[:, p * TILE:(p + 1) * TILE]
        qm_ref[2 * p] = jnp.where(first_half, qp, jnp.zeros_like(qp))
        qm_ref[2 * p + 1] = jnp.where(first_half, jnp.zeros_like(qp), qp)

    def key_tiles(kss, masks, accs, cbs):
        tiles = range(len(kss))
        kps = [[k_ref[pl.ds(ks, TILE), p * TILE:(p + 1) * TILE] for p in pairs] for ks in kss]
        vps = [[v_ref[pl.ds(ks, TILE), p * TILE:(p + 1) * TILE] for p in pairs] for ks in kss]
        qps = [qm_ref[2 * p:2 * p + 2].reshape(2 * TILE, TILE) for p in pairs]
        zps = [[_dot_nt(qps[p], kps[i][p]) for p in pairs] for i in tiles]
        zs, ts = [], []
        group = H_A // 2
        for i in tiles:
            zs.append([zps[i][h // 2][(h % 2) * TILE:(h % 2 + 1) * TILE] + bias_ref[h] for h in heads])
            ts.append([])
            for g0 in range(0, H_A, group):
                drops = [_softplus2(z) for z in zs[i][g0:g0 + group]]
                if masks[i] is not None:
                    drops = [jnp.where(masks[i], d, 0.0) for d in drops]
                tall = _dot(jnp.concatenate([d.astype(BF16) for d in drops], axis=0), uinc)
                ts[i] += [tall[h * TILE:(h + 1) * TILE] for h in range(group)]
        cbs = list(cbs)
        accs = list(accs)
        for i in tiles:
            ws = [jnp.exp2(zs[i][h] - ts[i][h] - cbs[h]) for h in heads]
            if masks[i] is not None:
                ws = [jnp.where(masks[i], a, 0.0) for a in ws]
            cbs = [cbs[h] + jnp.broadcast_to(ts[i][h][:, 0:1], (TILE, TILE)) for h in heads]
            for p in pairs:
                r = _dot(jnp.concatenate([ws[2 * p].astype(BF16), ws[2 * p + 1].astype(BF16)], axis=0), vps[i][p])
                accs[p] = accs[p] + jnp.where(first_half, r[:TILE], r[TILE:])
        return tuple(accs), tuple(cbs)

    zero = jnp.zeros((TILE, TILE), F32)
    carry = ((zero,) * (H_A // 2), (zero,) * H_A)

    def pair(ks, first_mask, carry):
        ks = pl.multiple_of(ks, TILE)
        return key_tiles([ks, pl.multiple_of(ks - TILE, TILE)], [first_mask, None], *carry)

    def quad(ks, carry):
        kss = [pl.multiple_of(ks - i * TILE, TILE) for i in range(4)]
        return key_tiles(kss, [None] * 4, *carry)

    carry = lax.cond(qi == 0, lambda c: key_tiles([0], [causal], *c), lambda c: pair(qi * TILE, causal, c), carry)
    rest = jnp.maximum(qi - 1, 0)
    carry = lax.fori_loop(0, rest // 4, lambda j, c: quad((qi - 2 - 4 * j) * TILE, c), carry)
    left = rest % 4
    carry = lax.cond(left >= 2, lambda c: pair((left - 1) * TILE, None, c), lambda c: c, carry)
    accs, _ = lax.cond(left % 2 == 1, lambda c: key_tiles([0], [None], *c), lambda c: c, carry)
    for p in pairs:
        o_ref[:, p * TILE:(p + 1) * TILE] = accs[p]


def _sb_prompt(bias, q_bf, k_bf, v_bf, uinc, nb, lp):
    nq = lp // TILE
    return pl.pallas_call(
        _sb_prompt_kernel,
        grid=(nb, nq),
        in_specs=[pl.BlockSpec(memory_space=pltpu.SMEM),
                  pl.BlockSpec((TILE, W_A), lambda b, i: (b * nq + i, 0)),
                  pl.BlockSpec((lp, W_A), lambda b, i: (b, 0)),
                  pl.BlockSpec((lp, W_A), lambda b, i: (b, 0)),
                  _full(uinc.shape)],
        out_specs=pl.BlockSpec((TILE, W_A), lambda b, i: (b * nq + i, 0)),
        out_shape=jax.ShapeDtypeStruct((nb * lp, W_A), F32),
        scratch_shapes=[pltpu.VMEM((H_A, TILE, TILE), BF16)],
        compiler_params=_cparams("parallel", "arbitrary"),
        name="sb_prompt",
    )(bias, q_bf, k_bf, v_bf, uinc)


def _sb_sample_kernel(pt_ref, q_ref, kn_ref, vn_ref, hm_ref, bias_ref, uinc_ref, *rest, n_pages, ts):
    kp_refs = rest[:n_pages]
    vp_refs = rest[n_pages:2 * n_pages]
    o_ref = rest[2 * n_pages]
    hm = hm_ref[...]
    qbd = q_ref[...] * hm
    bias = bias_ref[...]
    uinc = uinc_ref[...]
    rows = ts * H_A
    t_of_row = lax.broadcasted_iota(jnp.int32, (rows, 1), 0) // H_A
    out = jnp.zeros((rows, W_A), F32)
    carry = jnp.zeros((rows, 1), F32)
    for j in range(ts - 1, -1, -1):
        z = jnp.sum(qbd * kn_ref[j:j + 1, :], axis=-1, keepdims=True) + bias[:, 0:1]
        sp = _softplus2(z)
        m = j < t_of_row
        a = jnp.where(m, jnp.exp2(z - sp - carry), 0.0)
        carry = carry + jnp.where(m, sp, 0.0)
        out = out + a * vn_ref[j:j + 1, :]
    cb = jnp.broadcast_to(carry, (rows, TILE))
    qbd_bf = qbd.astype(BF16)
    pages = range(n_pages - 1, -1, -1)
    zs = [_dot(qbd_bf, kp_refs[p][...].astype(BF16)) + bias for p in pages]
    sps = [_softplus2(z) for z in zs]
    tall = _dot(jnp.concatenate([jnp.concatenate(_split2(sp), axis=1) for sp in sps], axis=0), uinc)
    for i, p in enumerate(pages):
        tail = tall[i * rows:(i + 1) * rows]
        a = jnp.exp2(zs[i] - tail - cb)
        cb = cb + jnp.broadcast_to(tail[:, 0:1], (rows, TILE))
        out = out + _dot_nt(a.astype(BF16), vp_refs[p][...].astype(BF16))
    o_ref[...] = jnp.sum((out * hm).reshape(ts, H_A, W_A), axis=1)


def _sb_sample(page_table, q_rep, kn, vn, hm, bias_rows, uinc, cache_k, cache_v):
    bs, n_pages = page_table.shape
    ts = kn.shape[1]
    page = cache_k.shape[2]
    rows = ts * H_A

    def page_spec(p):
        return pl.BlockSpec((None, W_A, page), lambda s, pt: (pt[s * n_pages + p], 0, 0))

    grid_spec = pltpu.PrefetchScalarGridSpec(
        num_scalar_prefetch=1,
        grid=(bs,),
        in_specs=[pl.BlockSpec((None, rows, W_A), lambda s, pt: (s, 0, 0)),
                  pl.BlockSpec((None, ts, W_A), lambda s, pt: (s, 0, 0)),
                  pl.BlockSpec((None, ts, W_A), lambda s, pt: (s, 0, 0)),
                  pl.BlockSpec(hm.shape, lambda s, pt: (0, 0)),
                  pl.BlockSpec(bias_rows.shape, lambda s, pt: (0, 0)),
                  pl.BlockSpec(uinc.shape, lambda s, pt: (0, 0))]
                 + [page_spec(p) for p in range(n_pages)] * 2,
        out_specs=pl.BlockSpec((None, ts, W_A), lambda s, pt: (s, 0, 0)),
    )
    return pl.pallas_call(
        functools.partial(_sb_sample_kernel, n_pages=n_pages, ts=ts),
        grid_spec=grid_spec,
        out_shape=jax.ShapeDtypeStruct((bs, ts, W_A), F32),
        compiler_params=_cparams("arbitrary"),
        name="sb_sample",
    )(page_table.reshape(-1), q_rep, kn, vn, hm, bias_rows, uinc,
      *([cache_k] * n_pages), *([cache_v] * n_pages))


def _tail0_math(o_a, g_a, u, win_sums, cnts, g_b, h, wp_ref, ps, wo_ref):
    parts = [(o_a * _silu(g_a)).astype(BF16)]
    for g in range(len(POOL_WINDOWS)):
        cs = slice(g * POOL_GC, (g + 1) * POOL_GC)
        diff = win_sums[g] / cnts[g] - u[:, cs]
        y = _dot(diff.astype(BF16), wp_ref[g]) * ps[:, cs]
        parts.append((y * _silu(g_b[:, cs])).astype(BF16))
    mixed = jnp.concatenate(parts, axis=1)
    return h + _dot(mixed, wo_ref[...])


def _tail0_prompt_kernel(oa_ref, ga_ref, u_ref, halo_ref, gb_ref, h_ref, wp_ref, ps_ref, wo_ref, out_ref,
                         ext_ref, *, tm, lp):
    i = pl.program_id(0)
    ext_ref[0:POOL_MAX, :] = halo_ref[...]
    ext_ref[POOL_MAX:POOL_MAX + tm, :] = u_ref[...]
    row = (i * tm) % lp + lax.broadcasted_iota(jnp.int32, (tm, 1), 0)
    pos = jnp.where(row >= lp, row - lp, row) - PAD
    win_sums, cnts = [], []
    for g, w in enumerate(POOL_WINDOWS):
        cs = slice(g * POOL_GC, (g + 1) * POOL_GC)
        s = ext_ref[:, cs]
        span = 1
        while span < w:
            s = s + pltpu.roll(s, span, axis=0)
            span *= 2
        win_sums.append(s[POOL_MAX:, :])
        cnts.append(jnp.clip(pos + 1, 1, w).astype(F32))
    out_ref[...] = _tail0_math(oa_ref[...], ga_ref[...], u_ref[...], win_sums, cnts, gb_ref[...], h_ref[...],
                               wp_ref, ps_ref[...], wo_ref)


def _tail0_prompt(o_a, g_a, u, g_b, h, wp_bf, ps, wo_bf, tm, lp):
    rows = h.shape[0]
    hb = tm // POOL_MAX
    row = lambda i: (i, 0)
    return pl.pallas_call(
        functools.partial(_tail0_prompt_kernel, tm=tm, lp=lp),
        grid=(rows // tm,),
        in_specs=[pl.BlockSpec((tm, W_A), row), pl.BlockSpec((tm, W_A), row), pl.BlockSpec((tm, W_B), row),
                  pl.BlockSpec((POOL_MAX, W_B), lambda i: (jnp.maximum(i * hb - 1, 0), 0)),
                  pl.BlockSpec((tm, W_B), row), pl.BlockSpec((tm, D_MODEL), row),
                  _full(wp_bf.shape), _full(ps.shape), _full(wo_bf.shape)],
        out_specs=pl.BlockSpec((tm, D_MODEL), row),
        out_shape=jax.ShapeDtypeStruct((rows, D_MODEL), F32),
        scratch_shapes=[pltpu.VMEM((POOL_MAX + tm, W_B), F32)],
        compiler_params=_cparams("parallel"),
        name="tail0_prompt",
    )(o_a, g_a, u, u, g_b, h, wp_bf, ps, wo_bf)


def _tail0_sample_kernel(oa_ref, ga_ref, ext_ref, gb_ref, h_ref, wp_ref, ps_ref, wo_ref, out_ref, *, ts, bs):
    pre = POOL_MAX - 1
    u = jnp.concatenate([ext_ref[pre + t] for t in range(ts)], axis=0)
    win_sums, cnts = [], []
    for g, w in enumerate(POOL_WINDOWS):
        cs = slice(g * POOL_GC, (g + 1) * POOL_GC)
        per_t = []
        for t in range(ts):
            s = ext_ref[pre + t, :, cs]
            for j in range(1, w):
                s = s + ext_ref[pre + t - j, :, cs]
            per_t.append(s)
        win_sums.append(jnp.concatenate(per_t, axis=0))
        cnts.append(float(w))
    out_ref[...] = _tail0_math(oa_ref[...], ga_ref[...], u, win_sums, cnts, gb_ref[...], h_ref[...],
                               wp_ref, ps_ref[...], wo_ref)


def _tail0_sample(o_a, g_a, ext_t, g_b, h, wp_bf, ps, wo_bf):
    rows = h.shape[0]
    ts = ext_t.shape[0] - (POOL_MAX - 1)
    args = (o_a, g_a, ext_t, g_b, h, wp_bf, ps, wo_bf)
    return pl.pallas_call(
        functools.partial(_tail0_sample_kernel, ts=ts, bs=ext_t.shape[1]),
        grid=(1,),
        in_specs=[_full(a.shape) for a in args],
        out_specs=_full((rows, D_MODEL)),
        out_shape=jax.ShapeDtypeStruct((rows, D_MODEL), F32),
        compiler_params=_cparams("arbitrary"),
        name="tail0_sample",
    )(*args)


def _qkv_math(ca, xm, wq_ref, wk_ref, wv_ref, q_ref, k_ref, v_ref):
    ca_bf = ca.astype(BF16)
    xm_bf = xm.astype(BF16)
    for h in range(H_C):
        cs = slice(h * DH_C, (h + 1) * DH_C)
        q_ref[:, cs] = _dot(ca_bf[:, cs], wq_ref[h]).astype(BF16)
        k_ref[:, cs] = (_dot(ca_bf[:, cs], wk_ref[h]) * (DH_C ** -0.5)).astype(BF16)
        v_ref[:, cs] = _dot(xm_bf[:, cs], wv_ref[h]).astype(BF16)


def _conv_qkv_prompt_kernel(xm_ref, halo_ref, cw_ref, cb_ref, wq_ref, wk_ref, wv_ref,
                            ca_ref, q_ref, k_ref, v_ref, ext_ref, *, tm):
    ext_ref[0:8, :] = halo_ref[...]
    ext_ref[8:8 + tm, :] = xm_ref[...]
    x = ext_ref[...]
    acc = x * cw_ref[0:1, :]
    for j in range(1, CONV_W):
        acc = pltpu.roll(acc, 1, axis=0) + x * cw_ref[j:j + 1, :]
    ca = _silu(acc[8:, :] + cb_ref[...])
    ca_ref[...] = ca
    _qkv_math(ca, xm_ref[...], wq_ref, wk_ref, wv_ref, q_ref, k_ref, v_ref)


def _conv_qkv_prompt(xm, cw, cb, wq_bf, wk_bf, wv_bf, tm):
    rows = xm.shape[0]
    row = lambda i: (i, 0)
    return pl.pallas_call(
        functools.partial(_conv_qkv_prompt_kernel, tm=tm),
        grid=(rows // tm,),
        in_specs=[pl.BlockSpec((tm, INNER_C), row),
                  pl.BlockSpec((8, INNER_C), lambda i: (jnp.maximum(i * (tm // 8) - 1, 0), 0)),
                  _full(cw.shape), _full(cb.shape), _full(wq_bf.shape), _full(wk_bf.shape), _full(wv_bf.shape)],
        out_specs=[pl.BlockSpec((tm, INNER_C), row)] * 4,
        out_shape=[jax.ShapeDtypeStruct((rows, INNER_C), F32)] + [jax.ShapeDtypeStruct((rows, INNER_C), BF16)] * 3,
        scratch_shapes=[pltpu.VMEM((8 + tm, INNER_C), F32)],
        compiler_params=_cparams("parallel"),
        name="conv_qkv_prompt",
    )(xm, xm, cw, cb, wq_bf, wk_bf, wv_bf)


def _conv_qkv_sample_kernel(ext_ref, cw_ref, cb_ref, wq_ref, wk_ref, wv_ref, ca_ref, q_ref, k_ref, v_ref):
    t = pl.program_id(0)
    bs = ext_ref.shape[1]
    conv = jnp.broadcast_to(cb_ref[...], (bs, INNER_C))
    for j in range(CONV_W):
        conv = conv + ext_ref[t + j] * cw_ref[j:j + 1, :]
    ca = _silu(conv)
    ca_ref[...] = ca
    _qkv_math(ca, ext_ref[t + CONV_W - 1], wq_ref, wk_ref, wv_ref, q_ref, k_ref, v_ref)


def _conv_qkv_sample(ext_t, cw, cb, wq_bf, wk_bf, wv_bf):
    ts = ext_t.shape[0] - (CONV_W - 1)
    bs = ext_t.shape[1]
    row = lambda t: (t, 0)
    return pl.pallas_call(
        _conv_qkv_sample_kernel,
        grid=(ts,),
        in_specs=[_full(ext_t.shape), _full(cw.shape), _full(cb.shape), _full(wq_bf.shape), _full(wk_bf.shape),
                  _full(wv_bf.shape)],
        out_specs=[pl.BlockSpec((bs, INNER_C), row)] * 4,
        out_shape=[jax.ShapeDtypeStruct((ts * bs, INNER_C), F32)]
                  + [jax.ShapeDtypeStruct((ts * bs, INNER_C), BF16)] * 3,
        compiler_params=_cparams("arbitrary"),
        name="conv_qkv_sample",
    )(ext_t, cw, cb, wq_bf, wk_bf, wv_bf)


def _mlstm_heads(qs, ks, vs, g, valid, Cs, ns, ms, lower, sel):
    nh = len(qs)
    H = range(nh)
    T = qs[0].shape[0]
    lane = lax.broadcasted_iota(jnp.int32, g.shape, 1)
    g = jnp.where(valid, g, jnp.where(lane < nh, NEG, 0.0))
    qks = [_dot_nt(qs[h], ks[h]) for h in H]
    qcs = [_dot(qs[h], Cs[h].astype(BF16)) for h in H]
    g_parts = _split3(g)
    cum = sum(_dot(lower, p) for p in g_parts)
    rows_g = sum(_dot_nt(sel, p) for p in g_parts)
    rows_c = sum(_dot_nt(sel, p) for p in _split3(cum))
    tri = lax.broadcasted_iota(jnp.int32, (T, T), 1) <= lax.broadcasted_iota(jnp.int32, (T, T), 0)
    b_cs = [cum[:, nh + h:nh + h + 1] for h in H]
    ds = [jnp.where(tri, b_cs[h] - rows_c[nh + h:nh + h + 1, :] + rows_g[h:h + 1, :], NEG) for h in H]
    m_inters = [b_cs[h] + ms[h] for h in H]
    m_ts = [jnp.maximum(m_inters[h], jnp.max(ds[h], axis=-1, keepdims=True)) for h in H]
    ss = [jnp.where(tri, qks[h] * jnp.exp(ds[h] - m_ts[h]), 0.0) for h in H]
    svs = [_dot(ss[h].astype(BF16), vs[h]) for h in H]
    m_news = [m_ts[h][T - 1:T, :] for h in H]
    b_lasts = [b_cs[h][T - 1:T, :] for h in H]
    kws = [ks[h].astype(F32) * jnp.exp(b_lasts[h] - b_cs[h] + g[:, h:h + 1] - m_news[h]) for h in H]
    upds = [_dot_tn(kws[h].astype(BF16), vs[h]) for h in H]
    decays = [jnp.exp(b_lasts[h] + ms[h] - m_news[h]) for h in H]
    C_news = [decays[h] * Cs[h] + upds[h] for h in H]
    n_news = [decays[h] * ns[h] + jnp.sum(kws[h], axis=0, keepdims=True) for h in H]
    hns = []
    for h in H:
        w_inter = jnp.exp(m_inters[h] - m_ts[h])
        num = w_inter * qcs[h] + svs[h]
        den = (w_inter * jnp.sum(qs[h].astype(F32) * ns[h], axis=-1, keepdims=True)
               + jnp.sum(ss[h], axis=-1, keepdims=True))
        hh = num / jnp.maximum(jnp.abs(den), jnp.exp(-m_ts[h]))
        hc = hh - jnp.mean(hh, axis=-1, keepdims=True)
        var = jnp.mean(hc * hc, axis=-1, keepdims=True)
        hns.append(hc * lax.rsqrt(var + EPS))
    return hns, C_news, n_news, m_news


def _mlstm_step(q_ref, k_ref, v_ref, g_ref, ca_ref, z_ref, og_ref, sk_ref, valid, c_in, n_in, m_in,
                lower_ref, sel_ref, mix_ref, c_ref, n_ref, m_ref):
    H = range(H_C)
    cols = [slice(h * DH_C, (h + 1) * DH_C) for h in H]
    hns, Cs, ns, ms = _mlstm_heads(
        [q_ref[:, c] for c in cols], [k_ref[:, c] for c in cols], [v_ref[:, c] for c in cols], g_ref[...], valid,
        [c_in[h] for h in H], [n_in[h] for h in H], [m_in[h][:, 0:1] for h in H], lower_ref[...], sel_ref[...])
    for h in H:
        c = cols[h]
        mix = (hns[h] * og_ref[:, c] + sk_ref[:, c] * ca_ref[:, c]) * _silu(z_ref[:, c])
        mix_ref[:, c] = mix.astype(mix_ref.dtype)
        c_ref[h] = Cs[h]
        n_ref[h] = ns[h]
        m_ref[h] = jnp.broadcast_to(ms[h], (1, TILE))


def _mlstm_prompt_kernel(q_ref, k_ref, v_ref, g_ref, ca_ref, z_ref, og_ref, sk_ref, lower_ref, sel_ref,
                         mix_ref, c_ref, n_ref, m_ref):
    c = pl.program_id(1)

    @pl.when(c == 0)
    def _():
        c_ref[...] = jnp.zeros_like(c_ref)
        n_ref[...] = jnp.zeros_like(n_ref)
        m_ref[...] = jnp.zeros_like(m_ref)

    valid = c * TILE + lax.broadcasted_iota(jnp.int32, (TILE, 1), 0) >= PAD
    _mlstm_step(q_ref, k_ref, v_ref, g_ref, ca_ref, z_ref, og_ref, sk_ref, valid, c_ref, n_ref, m_ref,
                lower_ref, sel_ref, mix_ref, c_ref, n_ref, m_ref)


def _mlstm_prompt(q, k, v, g, ca, z, og, sk, lower, sel, nb, lp):
    nc = lp // TILE
    blk = lambda b, c: (b * nc + c, 0)
    st = lambda b, c: (b, 0, 0)
    return pl.pallas_call(
        _mlstm_prompt_kernel,
        grid=(nb, nc),
        in_specs=[pl.BlockSpec((TILE, INNER_C), blk)] * 3
                 + [pl.BlockSpec((TILE, TILE), blk), pl.BlockSpec((TILE, INNER_C), blk),
                    pl.BlockSpec((TILE, INNER_C), blk), _full(og.shape), _full(sk.shape),
                    _full(lower.shape), _full(sel.shape)],
        out_specs=[pl.BlockSpec((TILE, INNER_C), blk), pl.BlockSpec((H_C, DH_C, DH_C), st),
                   pl.BlockSpec((H_C, 1, DH_C), st), pl.BlockSpec((H_C, 1, TILE), st)],
        out_shape=[jax.ShapeDtypeStruct((nb * lp, INNER_C), BF16),
                   jax.ShapeDtypeStruct((nb * H_C, DH_C, DH_C), F32),
                   jax.ShapeDtypeStruct((nb * H_C, 1, DH_C), F32),
                   jax.ShapeDtypeStruct((nb * H_C, 1, TILE), F32)],
        compiler_params=_cparams("parallel", "arbitrary"),
        name="mlstm_prompt",
    )(q, k, v, g, ca, z, og, sk, lower, sel)


def _mlstm_sample_kernel(q_ref, k_ref, v_ref, g_ref, ca_ref, z_ref, og_ref, sk_ref, c_in, n_in, m_in,
                         lower_ref, sel_ref, mix_ref, c_ref, n_ref, m_ref, *, ts):
    valid = lax.broadcasted_iota(jnp.int32, (q_ref.shape[0], 1), 0) < ts
    _mlstm_step(q_ref, k_ref, v_ref, g_ref, ca_ref, z_ref, og_ref, sk_ref, valid, c_in, n_in, m_in,
                lower_ref, sel_ref, mix_ref, c_ref, n_ref, m_ref)


def _mlstm_sample(q, k, v, g, ca, z, og, sk, c0, n0, m0, lower, sel, ts):
    bs, rp = q.shape[0], q.shape[1]
    blk = lambda s: (s, 0, 0)
    rows = pl.BlockSpec((None, rp, INNER_C), blk)
    state_specs = [pl.BlockSpec((H_C, DH_C, DH_C), blk), pl.BlockSpec((H_C, 1, DH_C), blk),
                   pl.BlockSpec((H_C, 1, TILE), blk)]
    return pl.pallas_call(
        functools.partial(_mlstm_sample_kernel, ts=ts),
        grid=(bs,),
        in_specs=[rows] * 3 + [pl.BlockSpec((None, rp, TILE), blk), rows, rows,
                               _full(og.shape), _full(sk.shape)]
                 + state_specs + [_full(lower.shape), _full(sel.shape)],
        out_specs=[rows] + state_specs,
        out_shape=[jax.ShapeDtypeStruct((bs, rp, INNER_C), BF16),
                   jax.ShapeDtypeStruct((bs * H_C, DH_C, DH_C), F32),
                   jax.ShapeDtypeStruct((bs * H_C, 1, DH_C), F32),
                   jax.ShapeDtypeStruct((bs * H_C, 1, TILE), F32)],
        compiler_params=_cparams("parallel"),
        name="mlstm_sample",
    )(q, k, v, g, ca, z, og, sk, c0, n0, m0, lower, sel)


def _tail1_kernel(mix_ref, h_ref, wo_ref, fg_ref, y_ref):
    h2 = h_ref[...] + _dot(mix_ref[...], wo_ref[...])
    ms = jnp.mean(h2 * h2, axis=-1, keepdims=True)
    y_ref[...] = h2 * lax.rsqrt(ms + EPS) * fg_ref[...]


def _tail1(mix, h, wo_bf, fg, grid, in_map, out_map, out_rows, tm, element_rows=False):
    dim = pl.Element if element_rows else (lambda n: n)
    return pl.pallas_call(
        _tail1_kernel,
        grid=grid,
        in_specs=[pl.BlockSpec((dim(tm), dim(INNER_C)), in_map), pl.BlockSpec((dim(tm), dim(D_MODEL)), in_map),
                  _full(wo_bf.shape), _full(fg.shape)],
        out_specs=pl.BlockSpec((tm, D_MODEL), out_map),
        out_shape=jax.ShapeDtypeStruct((out_rows, D_MODEL), F32),
        compiler_params=_cparams(*(["parallel"] * len(grid))),
        name="tail1",
    )(mix, h, wo_bf, fg)


def kernel(x_prompt, x_sample, cache_sb_k, cache_sb_v, page_table, state_pool, state_C, state_n, state_m, state_conv, meta_tokens, norm_g, final_norm_g, w_in_a, w_out_a, sb_bias, w_pool, pool_scale, w_in_c, b_gate_c, conv_w, conv_b, w_q, w_k, w_v, skip_c, outnorm_g, w_out_c):
    nb, seq, _ = x_prompt.shape
    bs, ts, _ = x_sample.shape
    lp = PAD + N_META + seq
    n_pool, page = cache_sb_k.shape[1], cache_sb_k.shape[2]
    tm = 4 * TILE

    r = jnp.arange(TILE)
    uinc1 = (r[:, None] >= r[None, :]).astype(BF16)
    uinc = jnp.concatenate([uinc1, uinc1], axis=0)
    lower = (r[None, :] <= r[:, None]).astype(BF16)
    sel = (jnp.arange(16)[:, None] == r[None, :]).astype(BF16)

    w_in_a_bf = w_in_a[0].astype(BF16)
    w_out_a_bf = w_out_a[0].astype(BF16)
    w_pool_bf = w_pool[0].astype(BF16)
    ps = pool_scale[0].reshape(1, W_B)
    g0 = norm_g[0].reshape(1, D_MODEL)
    q_scale = DH_A ** -0.5 * LOG2E
    bias2 = sb_bias[0] * LOG2E
    outs0 = [(0, W_A, q_scale, BF16),
             (W_A, W_A, 1.0, F32), (2 * W_A, W_A, 1.0, F32),
             (W_A, W_A, 1.0, BF16), (2 * W_A, W_A, 1.0, BF16),
             (3 * W_A, W_A, 1.0, F32),
             (4 * W_A, W_B, 1.0, F32), (4 * W_A + W_B, W_B, 1.0, F32)]

    meta = jnp.broadcast_to(meta_tokens.astype(x_prompt.dtype)[None], (nb, N_META, D_MODEL))
    h_p = jnp.concatenate([jnp.zeros((nb, PAD, D_MODEL), x_prompt.dtype), meta, x_prompt], axis=1)
    h_p = h_p.reshape(nb * lp, D_MODEL)
    q_bf, k_p, v_p, k_bf, v_bf, ga_p, u_p, gb_p = _norm_proj(h_p, g0, w_in_a_bf, outs0, tm)
    oa_p = _sb_prompt(bias2, q_bf, k_bf, v_bf, uinc1, nb, lp)
    h1_p = _tail0_prompt(oa_p, ga_p, u_p, gb_p, h_p, w_pool_bf, ps, w_out_a_bf, tm, lp)

    def to_tm(a):
        return a.transpose(1, 0, 2).reshape(ts * bs, a.shape[-1])

    def to_sm(a):
        return a.reshape(ts, bs, a.shape[-1]).transpose(1, 0, 2)

    h_s = to_tm(x_sample)
    outs0_s = [(0, W_A, q_scale, F32)] + outs0[1:3] + outs0[5:]
    q_s, k_s, v_s, ga_s, u_s, gb_s = _norm_proj(h_s, g0, w_in_a_bf, outs0_s, tm)
    q_rep = jnp.repeat(to_sm(q_s), H_A, axis=1)
    row_head = jnp.arange(ts * H_A) % H_A
    hm = (row_head[:, None] == (jnp.arange(W_A) // DH_A)[None, :]).astype(F32)
    bias_rows = jnp.broadcast_to(bias2[row_head][:, None], (ts * H_A, TILE))

    def pages_t(c):
        return c.transpose(0, 2, 3, 1).reshape(n_pool, W_A, page)

    oa_s = _sb_sample(page_table, q_rep, to_sm(k_s), to_sm(v_s), hm, bias_rows, uinc,
                      pages_t(cache_sb_k[0]), pages_t(cache_sb_v[0]))
    pool_ext = jnp.concatenate([state_pool[0].transpose(1, 0, 2), u_s.reshape(ts, bs, W_B)], axis=0)
    h1_s = _tail0_sample(to_tm(oa_s), ga_s, pool_ext, gb_s, h_s, w_pool_bf, ps, w_out_a_bf)

    g1 = norm_g[1].reshape(1, D_MODEL)
    w_in_c_bf = w_in_c[0][:, :2 * INNER_C].astype(BF16)
    n_gate = 2 * H_C
    wg_bf = jnp.pad(w_in_c[0][:, 2 * INNER_C:], ((0, 0), (0, TILE - n_gate))).astype(BF16)
    bg = jnp.pad(b_gate_c[0], (0, TILE - n_gate)).reshape(1, TILE)
    cw, cb = conv_w[0], conv_b[0].reshape(1, INNER_C)
    wq_bf, wk_bf, wv_bf = w_q[0].astype(BF16), w_k[0].astype(BF16), w_v[0].astype(BF16)
    og, sk = outnorm_g[0].reshape(1, INNER_C), skip_c[0].reshape(1, INNER_C)
    w_out_c_bf = w_out_c[0].astype(BF16)
    fg = final_norm_g.reshape(1, D_MODEL)

    xm_p, z_p, gate_p = _norm_proj1(h1_p, g1, w_in_c_bf, wg_bf, bg, tm)
    ca_p, q1_p, k1_p, v1_p = _conv_qkv_prompt(xm_p, cw, cb, wq_bf, wk_bf, wv_bf, tm)
    mix_p, C_p, n_p, m_p = _mlstm_prompt(q1_p, k1_p, v1_p, gate_p, ca_p, z_p, og, sk, lower, sel, nb, lp)
    y_prompt = _tail1(mix_p, h1_p, w_out_c_bf, fg, (nb, seq // tm),
                      lambda b, i: (pl.multiple_of(b * lp + TILE + i * tm, TILE), 0),
                      lambda b, i: (b * (seq // tm) + i, 0),
                      nb * seq, tm, element_rows=True)

    xm_s, z_s, gate_s = _norm_proj1(h1_s, g1, w_in_c_bf, wg_bf, bg, tm)
    conv_ext = jnp.concatenate([state_conv[0].transpose(1, 0, 2), xm_s.reshape(ts, bs, INNER_C)], axis=0)
    ca_s, q1_s, k1_s, v1_s = _conv_qkv_sample(conv_ext, cw, cb, wq_bf, wk_bf, wv_bf)
    rp = 16

    def to_rows(a):
        return jnp.pad(to_sm(a), ((0, 0), (0, rp - ts), (0, 0)))

    m0 = jnp.broadcast_to(state_m[0].reshape(bs * H_C, 1, 1), (bs * H_C, 1, TILE))
    mix_s, C_s, n_s, m_s = _mlstm_sample(to_rows(q1_s), to_rows(k1_s), to_rows(v1_s), to_rows(gate_s),
                                         to_rows(ca_s), to_rows(z_s), og, sk,
                                         state_C[0].reshape(bs * H_C, DH_C, DH_C),
                                         state_n[0].reshape(bs * H_C, 1, DH_C), m0, lower[:rp, :rp], sel, ts)
    y_s = _tail1(to_tm(mix_s[:, :ts]), h1_s, w_out_c_bf, fg, (ts * bs // tm,),
                 lambda i: (i, 0), lambda i: (i, 0), ts * bs, tm)

    def heads(a):
        return a.reshape(a.shape[:-1] + (H_A, DH_A))

    k3, v3, u3 = (a.reshape(nb, lp, -1) for a in (k_p, v_p, u_p))
    return (y_prompt.reshape(nb, seq, D_MODEL), to_sm(y_s),
            heads(k3[:, PAD:])[None], heads(v3[:, PAD:])[None],
            heads(to_sm(k_s))[None], heads(to_sm(v_s))[None],
            u3[:, lp - (POOL_MAX - 1):][None], pool_ext[ts:].transpose(1, 0, 2)[None],
            C_p.reshape(1, nb, H_C, DH_C, DH_C), C_s.reshape(1, bs, H_C, DH_C, DH_C),
            n_p.reshape(1, nb, H_C, DH_C), n_s.reshape(1, bs, H_C, DH_C),
            m_p[:, 0, 0].reshape(1, nb, H_C), m_s[:, 0, 0].reshape(1, bs, H_C),
            xm_p.reshape(nb, lp, INNER_C)[:, lp - (CONV_W - 1):][None], conv_ext[ts:].transpose(1, 0, 2)[None])
```

```python
import functools

import jax
import jax.numpy as jnp
from jax import lax
from jax.experimental import pallas as pl
from jax.experimental.pallas import tpu as pltpu

F32 = jnp.float32
BF16 = jnp.bfloat16

D_MODEL = 1024
N_META = 16
W_A = 512
H_A = 8
DH_A = 64
W_B = 512
POOL_WINDOWS = (2, 4, 8, 16)
POOL_GC = 128
POOL_MAX = 16
INNER_C = 2048
H_C = 4
DH_C = 512
CONV_W = 4
EPS = 1e-6

TILE = 128
PAD = TILE - N_META
NEG = -1e30
VMEM_LIMIT = 56 * 1024 * 1024


def _cparams(*sem):
    return pltpu.CompilerParams(dimension_semantics=sem, vmem_limit_bytes=VMEM_LIMIT)


def _silu(x):
    return x / (1.0 + jnp.exp(-x))


def _softplus(z):
    return jnp.maximum(z, 0.0) + jnp.log(1.0 + jnp.exp(-jnp.abs(z)))


def _split2(x):
    hi = x.astype(BF16)
    lo = (x - hi.astype(F32)).astype(BF16)
    return hi, lo


def _split3(x):
    hi = x.astype(BF16)
    r = x - hi.astype(F32)
    mid = r.astype(BF16)
    lo = (r - mid.astype(F32)).astype(BF16)
    return hi, mid, lo


def _dot(a, b):
    return jnp.dot(a, b, preferred_element_type=F32)


def _dot_nt(a, b):
    return lax.dot_general(a, b, (((1,), (1,)), ((), ())), preferred_element_type=F32)


def _dot_tn(a, b):
    return lax.dot_general(a, b, (((0,), (0,)), ((), ())), preferred_element_type=F32)


def _full(shape):
    nd = len(shape)
    return pl.BlockSpec(shape, lambda *_: (0,) * nd)


def _norm_proj_kernel(x_ref, g_ref, w_ref, *out_refs, outs):
    x = x_ref[...]
    ms = jnp.mean(x * x, axis=-1, keepdims=True)
    xn = (x * lax.rsqrt(ms + EPS) * g_ref[...]).astype(BF16)
    accs = {}
    for o_ref, (off, width, scale) in zip(out_refs, outs):
        if (off, width) not in accs:
            accs[(off, width)] = _dot(xn, w_ref[:, off:off + width])
        acc = accs[(off, width)]
        if scale != 1.0:
            acc = acc * scale
        o_ref[...] = acc.astype(o_ref.dtype)


def _norm_proj(x, g, w_bf, outs, tm):
    rows = x.shape[0]
    kern = functools.partial(_norm_proj_kernel, outs=[(o, w, s) for o, w, s, _ in outs])
    return pl.pallas_call(
        kern,
        grid=(rows // tm,),
        in_specs=[pl.BlockSpec((tm, D_MODEL), lambda i: (i, 0)), _full((1, D_MODEL)), _full(w_bf.shape)],
        out_specs=[pl.BlockSpec((tm, w), lambda i: (i, 0)) for _, w, _, _ in outs],
        out_shape=[jax.ShapeDtypeStruct((rows, w), dt) for _, w, _, dt in outs],
        compiler_params=_cparams("parallel"),
        name="norm_proj",
    )(x, g, w_bf)


def _norm_proj1_kernel(x_ref, g_ref, w_ref, wg_ref, bg_ref, xm_ref, z_ref, gate_ref):
    x = x_ref[...]
    ms = jnp.mean(x * x, axis=-1, keepdims=True)
    xn = (x * lax.rsqrt(ms + EPS) * g_ref[...]).astype(BF16)
    xm_ref[...] = _dot(xn, w_ref[:, :INNER_C])
    z_ref[...] = _dot(xn, w_ref[:, INNER_C:])
    gl = _dot(xn, wg_ref[...]) + bg_ref[...]
    lane = lax.broadcasted_iota(jnp.int32, gl.shape, 1)
    gate_ref[...] = jnp.where(lane >= H_C, -_softplus(-gl), gl)


def _norm_proj1(x, g, w_bf, wg_bf, bg, tm):
    rows = x.shape[0]
    return pl.pallas_call(
        _norm_proj1_kernel,
        grid=(rows // tm,),
        in_specs=[pl.BlockSpec((tm, D_MODEL), lambda i: (i, 0)), _full((1, D_MODEL)), _full(w_bf.shape),
                  _full(wg_bf.shape), _full(bg.shape)],
        out_specs=[pl.BlockSpec((tm, INNER_C), lambda i: (i, 0)), pl.BlockSpec((tm, INNER_C), lambda i: (i, 0)),
                   pl.BlockSpec((tm, TILE), lambda i: (i, 0))],
        out_shape=[jax.ShapeDtypeStruct((rows, INNER_C), F32), jax.ShapeDtypeStruct((rows, INNER_C), F32),
                   jax.ShapeDtypeStruct((rows, TILE), F32)],
        compiler_params=_cparams("parallel"),
        name="norm_proj1",
    )(x, g, w_bf, wg_bf, bg)


LOG2E = 1.4426950408889634


def _softplus2(y):
    return jnp.where(y > 64.0, y, jnp.log2(1.0 + jnp.exp2(y)))


def _sb_prompt_kernel(bias_ref, q_ref, k_ref, v_ref, uinc_ref, o_ref, qm_ref):
    qi = pl.program_id(1)
    uinc = uinc_ref[...]
    lane = lax.broadcasted_iota(jnp.int32, (TILE, TILE), 1)
    first_half = lane < DH_A
    causal = lane < lax.broadcasted_iota(jnp.int32, (TILE, TILE), 0)
    heads = range(H_A)
    pairs = range(H_A // 2)
    for p in pairs:
        qp = q_ref[:, p * TILE:(p + 1) * TILE]
        qm_ref[2 * p] = jnp.where(first_half, qp, jnp.zeros_like(qp))
        qm_ref[2 * p + 1] = jnp.where(first_half, jnp.zeros_like(qp), qp)

    def key_tiles(kss, masks, accs, cbs):
        tiles = range(len(kss))
        kps = [[k_ref[pl.ds(ks, TILE), p * TILE:(p + 1) * TILE] for p in pairs] for ks in kss]
        vps = [[v_ref[pl.ds(ks, TILE), p * TILE:(p + 1) * TILE] for p in pairs] for ks in kss]
        qps = [qm_ref[2 * p:2 * p + 2].reshape(2 * TILE, TILE) for p in pairs]
        zps = [[_dot_nt(qps[p], kps[i][p]) for p in pairs] for i in tiles]
        zs, ts = [], []
        group = H_A // 2
        for i in tiles:
            zs.append([zps[i][h // 2][(h % 2) * TILE:(h % 2 + 1) * TILE] + bias_ref[h] for h in heads])
            ts.append([])
            for g0 in range(0, H_A, group):
                drops = [_softplus2(z) for z in zs[i][g0:g0 + group]]
                if masks[i] is not None:
                    drops = [jnp.where(masks[i], d, 0.0) for d in drops]
                tall = _dot(jnp.concatenate([d.astype(BF16) for d in drops], axis=0), uinc)
                ts[i] += [tall[h * TILE:(h + 1) * TILE] for h in range(group)]
        cbs = list(cbs)
        accs = list(accs)
        for i in tiles:
            ws = [jnp.exp2(zs[i][h] - ts[i][h] - cbs[h]) for h in heads]
            if masks[i] is not None:
                ws = [jnp.where(masks[i], a, 0.0) for a in ws]
            cbs = [cbs[h] + jnp.broadcast_to(ts[i][h][:, 0:1], (TILE, TILE)) for h in heads]
            for p in pairs:
                r = _dot(jnp.concatenate([ws[2 * p].astype(BF16), ws[2 * p + 1].astype(BF16)], axis=0), vps[i][p])
                accs[p] = accs[p] + jnp.where(first_half, r[:TILE], r[TILE:])
        return tuple(accs), tuple(cbs)

    zero = jnp.zeros((TILE, TILE), F32)
    carry = ((zero,) * (H_A // 2), (zero,) * H_A)

    def pair(ks, first_mask, carry):
        ks = pl.multiple_of(ks, TILE)
        return key_tiles([ks, pl.multiple_of(ks - TILE, TILE)], [first_mask, None], *carry)

    def quad(ks, carry):
        kss = [pl.multiple_of(ks - i * TILE, TILE) for i in range(4)]
        return key_tiles(kss, [None] * 4, *carry)

    carry = lax.cond(qi == 0, lambda c: key_tiles([0], [causal], *c), lambda c: pair(qi * TILE, causal, c), carry)
    rest = jnp.maximum(qi - 1, 0)
    carry = lax.fori_loop(0, rest // 4, lambda j, c: quad((qi - 2 - 4 * j) * TILE, c), carry)
    left = rest % 4
    carry = lax.cond(left >= 2, lambda c: pair((left - 1) * TILE, None, c), lambda c: c, carry)
    accs, _ = lax.cond(left % 2 == 1, lambda c: key_tiles([0], [None], *c), lambda c: c, carry)
    for p in pairs:
        o_ref[:, p * TILE:(p + 1) * TILE] = accs[p]


def _sb_prompt(bias, q_bf, k_bf, v_bf, uinc, nb, lp):
    nq = lp // TILE
    return pl.pallas_call(
        _sb_prompt_kernel,
        grid=(nb, nq),
        in_specs=[pl.BlockSpec(memory_space=pltpu.SMEM),
                  pl.BlockSpec((TILE, W_A), lambda b, i: (b * nq + i, 0)),
                  pl.BlockSpec((lp, W_A), lambda b, i: (b, 0)),
                  pl.BlockSpec((lp, W_A), lambda b, i: (b, 0)),
                  _full(uinc.shape)],
        out_specs=pl.BlockSpec((TILE, W_A), lambda b, i: (b * nq + i, 0)),
        out_shape=jax.ShapeDtypeStruct((nb * lp, W_A), F32),
        scratch_shapes=[pltpu.VMEM((H_A, TILE, TILE), BF16)],
        compiler_params=_cparams("parallel", "arbitrary"),
        name="sb_prompt",
    )(bias, q_bf, k_bf, v_bf, uinc)


def _sb_sample_kernel(pt_ref, q_ref, kn_ref, vn_ref, hm_ref, bias_ref, uinc_ref, *rest, n_pages, ts):
    kp_refs = rest[:n_pages]
    vp_refs = rest[n_pages:2 * n_pages]
    o_ref = rest[2 * n_pages]
    hm = hm_ref[...]
    qbd = q_ref[...] * hm
    bias = bias_ref[...]
    uinc = uinc_ref[...]
    rows = ts * H_A
    t_of_row = lax.broadcasted_iota(jnp.int32, (rows, 1), 0) // H_A
    out = jnp.zeros((rows, W_A), F32)
    carry = jnp.zeros((rows, 1), F32)
    for j in range(ts - 1, -1, -1):
        z = jnp.sum(qbd * kn_ref[j:j + 1, :], axis=-1, keepdims=True) + bias[:, 0:1]
        sp = _softplus2(z)
        m = j < t_of_row
        a = jnp.where(m, jnp.exp2(z - sp - carry), 0.0)
        carry = carry + jnp.where(m, sp, 0.0)
        out = out + a * vn_ref[j:j + 1, :]
    cb = jnp.broadcast_to(carry, (rows, TILE))
    qbd_bf = qbd.astype(BF16)
    pages = range(n_pages - 1, -1, -1)
    zs = [_dot(qbd_bf, kp_refs[p][...].astype(BF16)) + bias for p in pages]
    sps = [_softplus2(z) for z in zs]
    tall = _dot(jnp.concatenate([jnp.concatenate(_split2(sp), axis=1) for sp in sps], axis=0), uinc)
    for i, p in enumerate(pages):
        tail = tall[i * rows:(i + 1) * rows]
        a = jnp.exp2(zs[i] - tail - cb)
        cb = cb + jnp.broadcast_to(tail[:, 0:1], (rows, TILE))
        out = out + _dot_nt(a.astype(BF16), vp_refs[p][...].astype(BF16))
    o_ref[...] = jnp.sum((out * hm).reshape(ts, H_A, W_A), axis=1)


def _sb_sample(page_table, q_rep, kn, vn, hm, bias_rows, uinc, cache_k, cache_v):
    bs, n_pages = page_table.shape
    ts = kn.shape[1]
    page = cache_k.shape[2]
    rows = ts * H_A

    def page_spec(p):
        return pl.BlockSpec((None, W_A, page), lambda s, pt: (pt[s * n_pages + p], 0, 0))

    grid_spec = pltpu.PrefetchScalarGridSpec(
        num_scalar_prefetch=1,
        grid=(bs,),
        in_specs=[pl.BlockSpec((None, rows, W_A), lambda s, pt: (s, 0, 0)),
                  pl.BlockSpec((None, ts, W_A), lambda s, pt: (s, 0, 0)),
                  pl.BlockSpec((None, ts, W_A), lambda s, pt: (s, 0, 0)),
                  pl.BlockSpec(hm.shape, lambda s, pt: (0, 0)),
                  pl.BlockSpec(bias_rows.shape, lambda s, pt: (0, 0)),
                  pl.BlockSpec(uinc.shape, lambda s, pt: (0, 0))]
                 + [page_spec(p) for p in range(n_pages)] * 2,
        out_specs=pl.BlockSpec((None, ts, W_A), lambda s, pt: (s, 0, 0)),
    )
    return pl.pallas_call(
        functools.partial(_sb_sample_kernel, n_pages=n_pages, ts=ts),
        grid_spec=grid_spec,
        out_shape=jax.ShapeDtypeStruct((bs, ts, W_A), F32),
        compiler_params=_cparams("arbitrary"),
        name="sb_sample",
    )(page_table.reshape(-1), q_rep, kn, vn, hm, bias_rows, uinc,
      *([cache_k] * n_pages), *([cache_v] * n_pages))


def _tail0_math(o_a, g_a, u, win_sums, cnts, g_b, h, wp_ref, ps, wo_ref):
    parts = [(o_a * _silu(g_a)).astype(BF16)]
    for g in range(len(POOL_WINDOWS)):
        cs = slice(g * POOL_GC, (g + 1) * POOL_GC)
        diff = win_sums[g] / cnts[g] - u[:, cs]
        y = _dot(diff.astype(BF16), wp_ref[g]) * ps[:, cs]
        parts.append((y * _silu(g_b[:, cs])).astype(BF16))
    mixed = jnp.concatenate(parts, axis=1)
    return h + _dot(mixed, wo_ref[...])


def _tail0_prompt_kernel(oa_ref, ga_ref, u_ref, halo_ref, gb_ref, h_ref, wp_ref, ps_ref, wo_ref, out_ref,
                         ext_ref, *, tm, lp):
    i = pl.program_id(0)
    ext_ref[0:POOL_MAX, :] = halo_ref[...]
    ext_ref[POOL_MAX:POOL_MAX + tm, :] = u_ref[...]
    row = (i * tm) % lp + lax.broadcasted_iota(jnp.int32, (tm, 1), 0)
    pos = jnp.where(row >= lp, row - lp, row) - PAD
    win_sums, cnts = [], []
    for g, w in enumerate(POOL_WINDOWS):
        cs = slice(g * POOL_GC, (g + 1) * POOL_GC)
        s = ext_ref[:, cs]
        span = 1
        while span < w:
            s = s + pltpu.roll(s, span, axis=0)
            span *= 2
        win_sums.append(s[POOL_MAX:, :])
        cnts.append(jnp.clip(pos + 1, 1, w).astype(F32))
    out_ref[...] = _tail0_math(oa_ref[...], ga_ref[...], u_ref[...], win_sums, cnts, gb_ref[...], h_ref[...],
                               wp_ref, ps_ref[...], wo_ref)


def _tail0_prompt(o_a, g_a, u, g_b, h, wp_bf, ps, wo_bf, tm, lp):
    rows = h.shape[0]
    hb = tm // POOL_MAX
    row = lambda i: (i, 0)
    return pl.pallas_call(
        functools.partial(_tail0_prompt_kernel, tm=tm, lp=lp),
        grid=(rows // tm,),
        in_specs=[pl.BlockSpec((tm, W_A), row), pl.BlockSpec((tm, W_A), row), pl.BlockSpec((tm, W_B), row),
                  pl.BlockSpec((POOL_MAX, W_B), lambda i: (jnp.maximum(i * hb - 1, 0), 0)),
                  pl.BlockSpec((tm, W_B), row), pl.BlockSpec((tm, D_MODEL), row),
                  _full(wp_bf.shape), _full(ps.shape), _full(wo_bf.shape)],
        out_specs=pl.BlockSpec((tm, D_MODEL), row),
        out_shape=jax.ShapeDtypeStruct((rows, D_MODEL), F32),
        scratch_shapes=[pltpu.VMEM((POOL_MAX + tm, W_B), F32)],
        compiler_params=_cparams("parallel"),
        name="tail0_prompt",
    )(o_a, g_a, u, u, g_b, h, wp_bf, ps, wo_bf)


def _tail0_sample_kernel(oa_ref, ga_ref, ext_ref, gb_ref, h_ref, wp_ref, ps_ref, wo_ref, out_ref, *, ts, bs):
    pre = POOL_MAX - 1
    u = jnp.concatenate([ext_ref[pre + t] for t in range(ts)], axis=0)
    win_sums, cnts = [], []
    for g, w in enumerate(POOL_WINDOWS):
        cs = slice(g * POOL_GC, (g + 1) * POOL_GC)
        per_t = []
        for t in range(ts):
            s = ext_ref[pre + t, :, cs]
            for j in range(1, w):
                s = s + ext_ref[pre + t - j, :, cs]
            per_t.append(s)
        win_sums.append(jnp.concatenate(per_t, axis=0))
        cnts.append(float(w))
    out_ref[...] = _tail0_math(oa_ref[...], ga_ref[...], u, win_sums, cnts, gb_ref[...], h_ref[...],
                               wp_ref, ps_ref[...], wo_ref)


def _tail0_sample(o_a, g_a, ext_t, g_b, h, wp_bf, ps, wo_bf):
    rows = h.shape[0]
    ts = ext_t.shape[0] - (POOL_MAX - 1)
    args = (o_a, g_a, ext_t, g_b, h, wp_bf, ps, wo_bf)
    return pl.pallas_call(
        functools.partial(_tail0_sample_kernel, ts=ts, bs=ext_t.shape[1]),
        grid=(1,),
        in_specs=[_full(a.shape) for a in args],
        out_specs=_full((rows, D_MODEL)),
        out_shape=jax.ShapeDtypeStruct((rows, D_MODEL), F32),
        compiler_params=_cparams("arbitrary"),
        name="tail0_sample",
    )(*args)


def _qkv_math(ca, xm, wq_ref, wk_ref, wv_ref, q_ref, k_ref, v_ref):
    ca_bf = ca.astype(BF16)
    xm_bf = xm.astype(BF16)
    for h in range(H_C):
        cs = slice(h * DH_C, (h + 1) * DH_C)
        q_ref[:, cs] = _dot(ca_bf[:, cs], wq_ref[h]).astype(BF16)
        k_ref[:, cs] = (_dot(ca_bf[:, cs], wk_ref[h]) * (DH_C ** -0.5)).astype(BF16)
        v_ref[:, cs] = _dot(xm_bf[:, cs], wv_ref[h]).astype(BF16)


def _conv_qkv_prompt_kernel(xm_ref, halo_ref, cw_ref, cb_ref, wq_ref, wk_ref, wv_ref,
                            ca_ref, q_ref, k_ref, v_ref, ext_ref, *, tm):
    ext_ref[0:8, :] = halo_ref[...]
    ext_ref[8:8 + tm, :] = xm_ref[...]
    x = ext_ref[...]
    acc = x * cw_ref[0:1, :]
    for j in range(1, CONV_W):
        acc = pltpu.roll(acc, 1, axis=0) + x * cw_ref[j:j + 1, :]
    ca = _silu(acc[8:, :] + cb_ref[...])
    ca_ref[...] = ca
    _qkv_math(ca, xm_ref[...], wq_ref, wk_ref, wv_ref, q_ref, k_ref, v_ref)


def _conv_qkv_prompt(xm, cw, cb, wq_bf, wk_bf, wv_bf, tm):
    rows = xm.shape[0]
    row = lambda i: (i, 0)
    return pl.pallas_call(
        functools.partial(_conv_qkv_prompt_kernel, tm=tm),
        grid=(rows // tm,),
        in_specs=[pl.BlockSpec((tm, INNER_C), row),
                  pl.BlockSpec((8, INNER_C), lambda i: (jnp.maximum(i * (tm // 8) - 1, 0), 0)),
                  _full(cw.shape), _full(cb.shape), _full(wq_bf.shape), _full(wk_bf.shape), _full(wv_bf.shape)],
        out_specs=[pl.BlockSpec((tm, INNER_C), row)] * 4,
        out_shape=[jax.ShapeDtypeStruct((rows, INNER_C), F32)] + [jax.ShapeDtypeStruct((rows, INNER_C), BF16)] * 3,
        scratch_shapes=[pltpu.VMEM((8 + tm, INNER_C), F32)],
        compiler_params=_cparams("parallel"),
        name="conv_qkv_prompt",
    )(xm, xm, cw, cb, wq_bf, wk_bf, wv_bf)


def _conv_qkv_sample_kernel(ext_ref, cw_ref, cb_ref, wq_ref, wk_ref, wv_ref, ca_ref, q_ref, k_ref, v_ref):
    t = pl.program_id(0)
    bs = ext_ref.shape[1]
    conv = jnp.broadcast_to(cb_ref[...], (bs, INNER_C))
    for j in range(CONV_W):
        conv = conv + ext_ref[t + j] * cw_ref[j:j + 1, :]
    ca = _silu(conv)
    ca_ref[...] = ca
    _qkv_math(ca, ext_ref[t + CONV_W - 1], wq_ref, wk_ref, wv_ref, q_ref, k_ref, v_ref)


def _conv_qkv_sample(ext_t, cw, cb, wq_bf, wk_bf, wv_bf):
    ts = ext_t.shape[0] - (CONV_W - 1)
    bs = ext_t.shape[1]
    row = lambda t: (t, 0)
    return pl.pallas_call(
        _conv_qkv_sample_kernel,
        grid=(ts,),
        in_specs=[_full(ext_t.shape), _full(cw.shape), _full(cb.shape), _full(wq_bf.shape), _full(wk_bf.shape),
                  _full(wv_bf.shape)],
        out_specs=[pl.BlockSpec((bs, INNER_C), row)] * 4,
        out_shape=[jax.ShapeDtypeStruct((ts * bs, INNER_C), F32)]
                  + [jax.ShapeDtypeStruct((ts * bs, INNER_C), BF16)] * 3,
        compiler_params=_cparams("arbitrary"),
        name="conv_qkv_sample",
    )(ext_t, cw, cb, wq_bf, wk_bf, wv_bf)


def _mlstm_heads(qs, ks, vs, g, valid, Cs, ns, ms, lower, sel):
    nh = len(qs)
    H = range(nh)
    T = qs[0].shape[0]
    lane = lax.broadcasted_iota(jnp.int32, g.shape, 1)
    g = jnp.where(valid, g, jnp.where(lane < nh, NEG, 0.0))
    qks = [_dot_nt(qs[h], ks[h]) for h in H]
    qcs = [_dot(qs[h], Cs[h].astype(BF16)) for h in H]
    g_parts = _split3(g)
    cum = sum(_dot(lower, p) for p in g_parts)
    rows_g = sum(_dot_nt(sel, p) for p in g_parts)
    rows_c = sum(_dot_nt(sel, p) for p in _split3(cum))
    tri = lax.broadcasted_iota(jnp.int32, (T, T), 1) <= lax.broadcasted_iota(jnp.int32, (T, T), 0)
    b_cs = [cum[:, nh + h:nh + h + 1] for h in H]
    ds = [jnp.where(tri, b_cs[h] - rows_c[nh + h:nh + h + 1, :] + rows_g[h:h + 1, :], NEG) for h in H]
    m_inters = [b_cs[h] + ms[h] for h in H]
    m_ts = [jnp.maximum(m_inters[h], jnp.max(ds[h], axis=-1, keepdims=True)) for h in H]
    ss = [jnp.where(tri, qks[h] * jnp.exp(ds[h] - m_ts[h]), 0.0) for h in H]
    svs = [_dot(ss[h].astype(BF16), vs[h]) for h in H]
    m_news = [m_ts[h][T - 1:T, :] for h in H]
    b_lasts = [b_cs[h][T - 1:T, :] for h in H]
    kws = [ks[h].astype(F32) * jnp.exp(b_lasts[h] - b_cs[h] + g[:, h:h + 1] - m_news[h]) for h in H]
    upds = [_dot_tn(kws[h].astype(BF16), vs[h]) for h in H]
    decays = [jnp.exp(b_lasts[h] + ms[h] - m_news[h]) for h in H]
    C_news = [decays[h] * Cs[h] + upds[h] for h in H]
    n_news = [decays[h] * ns[h] + jnp.sum(kws[h], axis=0, keepdims=True) for h in H]
    hns = []
    for h in H:
        w_inter = jnp.exp(m_inters[h] - m_ts[h])
        num = w_inter * qcs[h] + svs[h]
        den = (w_inter * jnp.sum(qs[h].astype(F32) * ns[h], axis=-1, keepdims=True)
               + jnp.sum(ss[h], axis=-1, keepdims=True))
        hh = num / jnp.maximum(jnp.abs(den), jnp.exp(-m_ts[h]))
        hc = hh - jnp.mean(hh, axis=-1, keepdims=True)
        var = jnp.mean(hc * hc, axis=-1, keepdims=True)
        hns.append(hc * lax.rsqrt(var + EPS))
    return hns, C_news, n_news, m_news


def _mlstm_step(q_ref, k_ref, v_ref, g_ref, ca_ref, z_ref, og_ref, sk_ref, valid, c_in, n_in, m_in,
                lower_ref, sel_ref, mix_ref, c_ref, n_ref, m_ref):
    H = range(H_C)
    cols = [slice(h * DH_C, (h + 1) * DH_C) for h in H]
    hns, Cs, ns, ms = _mlstm_heads(
        [q_ref[:, c] for c in cols], [k_ref[:, c] for c in cols], [v_ref[:, c] for c in cols], g_ref[...], valid,
        [c_in[h] for h in H], [n_in[h] for h in H], [m_in[h][:, 0:1] for h in H], lower_ref[...], sel_ref[...])
    for h in H:
        c = cols[h]
        mix = (hns[h] * og_ref[:, c] + sk_ref[:, c] * ca_ref[:, c]) * _silu(z_ref[:, c])
        mix_ref[:, c] = mix.astype(mix_ref.dtype)
        c_ref[h] = Cs[h]
        n_ref[h] = ns[h]
        m_ref[h] = jnp.broadcast_to(ms[h], (1, TILE))


def _mlstm_prompt_kernel(q_ref, k_ref, v_ref, g_ref, ca_ref, z_ref, og_ref, sk_ref, lower_ref, sel_ref,
                         mix_ref, c_ref, n_ref, m_ref):
    c = pl.program_id(1)

    @pl.when(c == 0)
    def _():
        c_ref[...] = jnp.zeros_like(c_ref)
        n_ref[...] = jnp.zeros_like(n_ref)
        m_ref[...] = jnp.zeros_like(m_ref)

    valid = c * TILE + lax.broadcasted_iota(jnp.int32, (TILE, 1), 0) >= PAD
    _mlstm_step(q_ref, k_ref, v_ref, g_ref, ca_ref, z_ref, og_ref, sk_ref, valid, c_ref, n_ref, m_ref,
                lower_ref, sel_ref, mix_ref, c_ref, n_ref, m_ref)


def _mlstm_prompt(q, k, v, g, ca, z, og, sk, lower, sel, nb, lp):
    nc = lp // TILE
    blk = lambda b, c: (b * nc + c, 0)
    st = lambda b, c: (b, 0, 0)
    return pl.pallas_call(
        _mlstm_prompt_kernel,
        grid=(nb, nc),
        in_specs=[pl.BlockSpec((TILE, INNER_C), blk)] * 3
                 + [pl.BlockSpec((TILE, TILE), blk), pl.BlockSpec((TILE, INNER_C), blk),
                    pl.BlockSpec((TILE, INNER_C), blk), _full(og.shape), _full(sk.shape),
                    _full(lower.shape), _full(sel.shape)],
        out_specs=[pl.BlockSpec((TILE, INNER_C), blk), pl.BlockSpec((H_C, DH_C, DH_C), st),
                   pl.BlockSpec((H_C, 1, DH_C), st), pl.BlockSpec((H_C, 1, TILE), st)],
        out_shape=[jax.ShapeDtypeStruct((nb * lp, INNER_C), BF16),
                   jax.ShapeDtypeStruct((nb * H_C, DH_C, DH_C), F32),
                   jax.ShapeDtypeStruct((nb * H_C, 1, DH_C), F32),
                   jax.ShapeDtypeStruct((nb * H_C, 1, TILE), F32)],
        compiler_params=_cparams("parallel", "arbitrary"),
        name="mlstm_prompt",
    )(q, k, v, g, ca, z, og, sk, lower, sel)


def _mlstm_sample_kernel(q_ref, k_ref, v_ref, g_ref, ca_ref, z_ref, og_ref, sk_ref, c_in, n_in, m_in,
                         lower_ref, sel_ref, mix_ref, c_ref, n_ref, m_ref, *, ts):
    valid = lax.broadcasted_iota(jnp.int32, (q_ref.shape[1], 1), 0) < ts
    for s in range(q_ref.shape[0]):
        st = slice(s * H_C, (s + 1) * H_C)
        _mlstm_step(q_ref.at[s], k_ref.at[s], v_ref.at[s], g_ref.at[s], ca_ref.at[s], z_ref.at[s], og_ref, sk_ref,
                    valid, c_in.at[st], n_in.at[st], m_in.at[st], lower_ref, sel_ref, mix_ref.at[s],
                    c_ref.at[st], n_ref.at[st], m_ref.at[st])


def _mlstm_sample(q, k, v, g, ca, z, og, sk, c0, n0, m0, lower, sel, ts, per_step=2):
    bs, rp = q.shape[0], q.shape[1]
    blk = lambda s: (s, 0, 0)
    rows = pl.BlockSpec((per_step, rp, INNER_C), blk)
    nst = per_step * H_C
    state_specs = [pl.BlockSpec((nst, DH_C, DH_C), blk), pl.BlockSpec((nst, 1, DH_C), blk),
                   pl.BlockSpec((nst, 1, TILE), blk)]
    return pl.pallas_call(
        functools.partial(_mlstm_sample_kernel, ts=ts),
        grid=(bs // per_step,),
        in_specs=[rows] * 3 + [pl.BlockSpec((per_step, rp, TILE), blk), rows, rows,
                               _full(og.shape), _full(sk.shape)]
                 + state_specs + [_full(lower.shape), _full(sel.shape)],
        out_specs=[rows] + state_specs,
        out_shape=[jax.ShapeDtypeStruct((bs, rp, INNER_C), BF16),
                   jax.ShapeDtypeStruct((bs * H_C, DH_C, DH_C), F32),
                   jax.ShapeDtypeStruct((bs * H_C, 1, DH_C), F32),
                   jax.ShapeDtypeStruct((bs * H_C, 1, TILE), F32)],
        compiler_params=_cparams("parallel"),
        name="mlstm_sample",
    )(q, k, v, g, ca, z, og, sk, c0, n0, m0, lower, sel)


def _tail1_kernel(mix_ref, h_ref, wo_ref, fg_ref, y_ref):
    h2 = h_ref[...] + _dot(mix_ref[...], wo_ref[...])
    ms = jnp.mean(h2 * h2, axis=-1, keepdims=True)
    y_ref[...] = h2 * lax.rsqrt(ms + EPS) * fg_ref[...]


def _tail1(mix, h, wo_bf, fg, grid, in_map, out_map, out_rows, tm, element_rows=False):
    dim = pl.Element if element_rows else (lambda n: n)
    return pl.pallas_call(
        _tail1_kernel,
        grid=grid,
        in_specs=[pl.BlockSpec((dim(tm), dim(INNER_C)), in_map), pl.BlockSpec((dim(tm), dim(D_MODEL)), in_map),
                  _full(wo_bf.shape), _full(fg.shape)],
        out_specs=pl.BlockSpec((tm, D_MODEL), out_map),
        out_shape=jax.ShapeDtypeStruct((out_rows, D_MODEL), F32),
        compiler_params=_cparams(*(["parallel"] * len(grid))),
        name="tail1",
    )(mix, h, wo_bf, fg)


def kernel(x_prompt, x_sample, cache_sb_k, cache_sb_v, page_table, state_pool, state_C, state_n, state_m, state_conv, meta_tokens, norm_g, final_norm_g, w_in_a, w_out_a, sb_bias, w_pool, pool_scale, w_in_c, b_gate_c, conv_w, conv_b, w_q, w_k, w_v, skip_c, outnorm_g, w_out_c):
    nb, seq, _ = x_prompt.shape
    bs, ts, _ = x_sample.shape
    lp = PAD + N_META + seq
    n_pool, page = cache_sb_k.shape[1], cache_sb_k.shape[2]
    tm = 4 * TILE

    r = jnp.arange(TILE)
    uinc1 = (r[:, None] >= r[None, :]).astype(BF16)
    uinc = jnp.concatenate([uinc1, uinc1], axis=0)
    lower = (r[None, :] <= r[:, None]).astype(BF16)
    sel = (jnp.arange(16)[:, None] == r[None, :]).astype(BF16)

    w_in_a_bf = w_in_a[0].astype(BF16)
    w_out_a_bf = w_out_a[0].astype(BF16)
    w_pool_bf = w_pool[0].astype(BF16)
    ps = pool_scale[0].reshape(1, W_B)
    g0 = norm_g[0].reshape(1, D_MODEL)
    q_scale = DH_A ** -0.5 * LOG2E
    bias2 = sb_bias[0] * LOG2E
    outs0 = [(0, W_A, q_scale, BF16),
             (W_A, W_A, 1.0, F32), (2 * W_A, W_A, 1.0, F32),
             (W_A, W_A, 1.0, BF16), (2 * W_A, W_A, 1.0, BF16),
             (3 * W_A, W_A, 1.0, F32),
             (4 * W_A, W_B, 1.0, F32), (4 * W_A + W_B, W_B, 1.0, F32)]

    meta = jnp.broadcast_to(meta_tokens.astype(x_prompt.dtype)[None], (nb, N_META, D_MODEL))
    h_p = jnp.concatenate([jnp.zeros((nb, PAD, D_MODEL), x_prompt.dtype), meta, x_prompt], axis=1)
    h_p = h_p.reshape(nb * lp, D_MODEL)
    q_bf, k_p, v_p, k_bf, v_bf, ga_p, u_p, gb_p = _norm_proj(h_p, g0, w_in_a_bf, outs0, tm)
    oa_p = _sb_prompt(bias2, q_bf, k_bf, v_bf, uinc1, nb, lp)
    h1_p = _tail0_prompt(oa_p, ga_p, u_p, gb_p, h_p, w_pool_bf, ps, w_out_a_bf, tm, lp)

    def to_tm(a):
        return a.transpose(1, 0, 2).reshape(ts * bs, a.shape[-1])

    def to_sm(a):
        return a.reshape(ts, bs, a.shape[-1]).transpose(1, 0, 2)

    h_s = to_tm(x_sample)
    outs0_s = [(0, W_A, q_scale, F32)] + outs0[1:3] + outs0[5:]
    q_s, k_s, v_s, ga_s, u_s, gb_s = _norm_proj(h_s, g0, w_in_a_bf, outs0_s, tm)
    q_rep = jnp.repeat(to_sm(q_s), H_A, axis=1)
    row_head = jnp.arange(ts * H_A) % H_A
    hm = (row_head[:, None] == (jnp.arange(W_A) // DH_A)[None, :]).astype(F32)
    bias_rows = jnp.broadcast_to(bias2[row_head][:, None], (ts * H_A, TILE))

    def pages_t(c):
        return c.transpose(0, 2, 3, 1).reshape(n_pool, W_A, page)

    oa_s = _sb_sample(page_table, q_rep, to_sm(k_s), to_sm(v_s), hm, bias_rows, uinc,
                      pages_t(cache_sb_k[0]), pages_t(cache_sb_v[0]))
    pool_ext = jnp.concatenate([state_pool[0].transpose(1, 0, 2), u_s.reshape(ts, bs, W_B)], axis=0)
    h1_s = _tail0_sample(to_tm(oa_s), ga_s, pool_ext, gb_s, h_s, w_pool_bf, ps, w_out_a_bf)

    g1 = norm_g[1].reshape(1, D_MODEL)
    w_in_c_bf = w_in_c[0][:, :2 * INNER_C].astype(BF16)
    n_gate = 2 * H_C
    wg_bf = jnp.pad(w_in_c[0][:, 2 * INNER_C:], ((0, 0), (0, TILE - n_gate))).astype(BF16)
    bg = jnp.pad(b_gate_c[0], (0, TILE - n_gate)).reshape(1, TILE)
    cw, cb = conv_w[0], conv_b[0].reshape(1, INNER_C)
    wq_bf, wk_bf, wv_bf = w_q[0].astype(BF16), w_k[0].astype(BF16), w_v[0].astype(BF16)
    og, sk = outnorm_g[0].reshape(1, INNER_C), skip_c[0].reshape(1, INNER_C)
    w_out_c_bf = w_out_c[0].astype(BF16)
    fg = final_norm_g.reshape(1, D_MODEL)

    xm_p, z_p, gate_p = _norm_proj1(h1_p, g1, w_in_c_bf, wg_bf, bg, tm)
    ca_p, q1_p, k1_p, v1_p = _conv_qkv_prompt(xm_p, cw, cb, wq_bf, wk_bf, wv_bf, tm)
    mix_p, C_p, n_p, m_p = _mlstm_prompt(q1_p, k1_p, v1_p, gate_p, ca_p, z_p, og, sk, lower, sel, nb, lp)
    y_prompt = _tail1(mix_p, h1_p, w_out_c_bf, fg, (nb, seq // tm),
                      lambda b, i: (pl.multiple_of(b * lp + TILE + i * tm, TILE), 0),
                      lambda b, i: (b * (seq // tm) + i, 0),
                      nb * seq, tm, element_rows=True)

    xm_s, z_s, gate_s = _norm_proj1(h1_s, g1, w_in_c_bf, wg_bf, bg, tm)
    conv_ext = jnp.concatenate([state_conv[0].transpose(1, 0, 2), xm_s.reshape(ts, bs, INNER_C)], axis=0)
    ca_s, q1_s, k1_s, v1_s = _conv_qkv_sample(conv_ext, cw, cb, wq_bf, wk_bf, wv_bf)
    rp = 16

    def to_rows(a):
        return jnp.pad(to_sm(a), ((0, 0), (0, rp - ts), (0, 0)))

    m0 = jnp.broadcast_to(state_m[0].reshape(bs * H_C, 1, 1), (bs * H_C, 1, TILE))
    mix_s, C_s, n_s, m_s = _mlstm_sample(to_rows(q1_s), to_rows(k1_s), to_rows(v1_s), to_rows(gate_s),
                                         to_rows(ca_s), to_rows(z_s), og, sk,
                                         state_C[0].reshape(bs * H_C, DH_C, DH_C),
                                         state_n[0].reshape(bs * H_C, 1, DH_C), m0, lower[:rp, :rp], sel, ts)
    y_s = _tail1(to_tm(mix_s[:, :ts]), h1_s, w_out_c_bf, fg, (ts * bs // tm,),
                 lambda i: (i, 0), lambda i: (i, 0), ts * bs, tm)

    def heads(a):
        return a.reshape(a.shape[:-1] + (H_A, DH_A))

    k3, v3, u3 = (a.reshape(nb, lp, -1) for a in (k_p, v_p, u_p))
    return (y_prompt.reshape(nb, seq, D_MODEL), to_sm(y_s),
            heads(k3[:, PAD:])[None], heads(v3[:, PAD:])[None],
            heads(to_sm(k_s))[None], heads(to_sm(v_s))[None],
            u3[:, lp - (POOL_MAX - 1):][None], pool_ext[ts:].transpose(1, 0, 2)[None],
            C_p.reshape(1, nb, H_C, DH_C, DH_C), C_s.reshape(1, bs, H_C, DH_C, DH_C),
            n_p.reshape(1, nb, H_C, DH_C), n_s.reshape(1, bs, H_C, DH_C),
            m_p[:, 0, 0].reshape(1, nb, H_C), m_s[:, 0, 0].reshape(1, bs, H_C),
            xm_p.reshape(nb, lp, INNER_C)[:, lp - (CONV_W - 1):][None], conv_ext[ts:].transpose(1, 0, 2)[None])
```
